```python
import math
import jax, jax.numpy as jnp
from jax import lax
import numpy as np

D_MODEL = 1024
BATCH = 1
SEQ = 16384
DEPTH = 2
DEC_BATCH = 128
DEC_SEQ = 1
PAST_LEN = 16384
PAGE_SIZE = 128

N_MIXERS = 2
N_RG_LAYERS = (DEPTH + 1) // N_MIXERS
N_MLA_LAYERS = DEPTH // N_MIXERS
EPS = 1e-6
D_RNN = D_MODEL
RG_BLOCKS = 8
RG_BLOCK_W = D_RNN // RG_BLOCKS
CONV_W = 4
RG_C = 8.0
MLA_HEADS = 8
Q_LORA = 512
KV_LORA = 256
QK_NOPE = 128
QK_ROPE = 64
V_HEAD = 128
ROPE_THETA = 10000.0
MLA_SCALE = (QK_NOPE + QK_ROPE) ** -0.5
Q_BLOCK = 128
N_MEM = 256
MEM_HEADS = 4
MEM_HD = 128
MEM_W = MEM_HEADS * MEM_HD
D_FF = ((8 * D_MODEL + 3 * 256 - 1) // (3 * 256)) * 256
RG_IN = 2 * D_RNN + MEM_W
MLA_IN = Q_LORA + KV_LORA + QK_ROPE + MEM_W

kernel_name = 'hybrid_rglru_mla_memxattn_decode_step'


def rms_norm(x, g):
    xf = x.astype(jnp.float32)
    y = xf * lax.rsqrt(jnp.mean(xf * xf, axis=-1, keepdims=True) + EPS)
    return (y * g.astype(jnp.float32)).astype(x.dtype)


def rope_angles(pos):
    inv = ROPE_THETA ** (-jnp.arange(0, QK_ROPE, 2, dtype=jnp.float32) / QK_ROPE)
    ang = pos.astype(jnp.float32)[:, None] * inv[None, :]
    return jnp.cos(ang), jnp.sin(ang)


def apply_rope(x, cos, sin):
    xf = x.astype(jnp.float32)
    half = QK_ROPE // 2
    x1, x2 = xf[..., :half], xf[..., half:]
    return jnp.concatenate([x1 * cos - x2 * sin, x2 * cos + x1 * sin], axis=-1).astype(x.dtype)


def causal_conv(xb, prev, w, b):
    s = xb.shape[1]
    xp = jnp.concatenate([prev.astype(xb.dtype), xb], axis=1)
    y = xp[:, 0:s] * w[0]
    for k in range(1, CONV_W):
        y = y + xp[:, k:k + s] * w[k]
    return y + b, xp[:, -(CONV_W - 1):]


def _lin_combine(c1, c2):
    a1, b1 = c1
    a2, b2 = c2
    return a1 * a2, a2 * b1 + b2


def rg_lru(xc, h0, wa, ba, wx, bx, lam):
    bsz, s, _ = xc.shape
    xblk = xc.reshape(bsz, s, RG_BLOCKS, RG_BLOCK_W)
    r = jax.nn.sigmoid(jnp.einsum('bsnc,ncd->bsnd', xblk, wa).reshape(bsz, s, D_RNN) + ba)
    i = jax.nn.sigmoid(jnp.einsum('bsnc,ncd->bsnd', xblk, wx).reshape(bsz, s, D_RNN) + bx)
    log_a = -RG_C * r.astype(jnp.float32) * jax.nn.softplus(-lam.astype(jnp.float32))
    a = jnp.exp(log_a)
    u = jnp.sqrt(-jnp.expm1(2.0 * log_a)) * (i * xc).astype(jnp.float32)
    u = u.at[:, 0].add(a[:, 0] * h0.astype(jnp.float32))
    _, h = lax.associative_scan(_lin_combine, (a, u), axis=1)
    return h.astype(xc.dtype), h[:, -1].astype(xc.dtype)


def mem_attend(q, k, v):
    bsz, s = q.shape[0], q.shape[1]
    sc = jnp.einsum('bshd,bmhd->bhsm', q, k).astype(jnp.float32) * (MEM_HD ** -0.5)
    p = jax.nn.softmax(sc, axis=-1).astype(v.dtype)
    return jnp.einsum('bhsm,bmhd->bshd', p, v).reshape(bsz, s, MEM_W)


def mla_attend(q_lat, q_pe, ckv, kpe, q_pos, k_pos):
    sc = (jnp.einsum('shc,tc->hst', q_lat, ckv) + jnp.einsum('shr,tr->hst', q_pe, kpe)).astype(jnp.float32) * MLA_SCALE
    sc = jnp.where(k_pos[None, None, :] <= q_pos[None, :, None], sc, -jnp.inf)
    p = jax.nn.softmax(sc, axis=-1).astype(ckv.dtype)
    return jnp.einsum('hst,tc->shc', p, ckv)


def swiglu(h, w_gu, w_down):
    gu = h @ w_gu
    return (jax.nn.silu(gu[..., :D_FF]) * gu[..., D_FF:]) @ w_down


def rg_sublayer(h, mk, mv, conv_prev, h0, w_in, conv_w, conv_b, wa, ba, wx, bx, lam, w_out):
    bsz, s, _ = h.shape
    z = h @ w_in
    gate, xb, mq = z[..., :D_RNN], z[..., D_RNN:2 * D_RNN], z[..., 2 * D_RNN:]
    xc, conv_new = causal_conv(xb, conv_prev, conv_w, conv_b)
    hseq, h_last = rg_lru(xc, h0, wa, ba, wx, bx, lam)
    y_rg = hseq * jax.nn.gelu(gate)
    y_mem = mem_attend(mq.reshape(bsz, s, MEM_HEADS, MEM_HD), mk, mv)
    out = jnp.concatenate([y_rg, y_mem], axis=-1) @ w_out
    return out, conv_new, h_last


def mla_sublayer(h, mk, mv, pos, attend, w_in, q_norm, kv_norm, w_q_up, w_uk, w_uv, w_out):
    bsz, s, _ = h.shape
    z = h @ w_in
    o1, o2, o3 = Q_LORA, Q_LORA + KV_LORA, Q_LORA + KV_LORA + QK_ROPE
    cq = rms_norm(z[..., :o1], q_norm)
    ckv = rms_norm(z[..., o1:o2], kv_norm)
    cos, sin = rope_angles(pos)
    kpe = apply_rope(z[..., o2:o3], cos, sin)
    mq = z[..., o3:]
    q = jnp.einsum('bsq,qhd->bshd', cq, w_q_up)
    q_pe = apply_rope(q[..., QK_NOPE:], cos[:, None, :], sin[:, None, :])
    q_lat = jnp.einsum('bshn,chn->bshc', q[..., :QK_NOPE], w_uk)
    o_lat = attend(q_lat, q_pe, ckv, kpe)
    y_att = jnp.einsum('bshc,chv->bshv', o_lat, w_uv).reshape(bsz, s, MLA_HEADS * V_HEAD)
    y_mem = mem_attend(mq.reshape(bsz, s, MEM_HEADS, MEM_HD), mk, mv)
    out = jnp.concatenate([y_att, y_mem], axis=-1) @ w_out
    return out, ckv, kpe


def setup_inputs(seed: int = 0) -> dict:
    key = jax.random.key(seed)
    ks = iter(jax.random.split(key, 48))
    f32 = jnp.float32

    def w(shape, fan_in):
        return jax.random.normal(next(ks), shape, f32) * (fan_in ** -0.5)

    def gain(shape):
        return 1.0 + 0.05 * jax.random.normal(next(ks), shape, f32)

    def bias(shape):
        return 0.01 * jax.random.normal(next(ks), shape, f32)

    def rnd(shape, scale=1.0):
        return scale * jax.random.normal(next(ks), shape, f32)

    n_pages = PAST_LEN // PAGE_SIZE
    n_used = DEC_BATCH * n_pages
    n_pool = n_used + max(1, n_used // 4)
    page_table = jax.random.permutation(next(ks), n_pool)[:n_used].reshape(DEC_BATCH, n_pages).astype(jnp.int32)
    a0 = jax.random.uniform(next(ks), (N_RG_LAYERS, D_RNN), f32, 0.9, 0.999)
    sig = a0 ** (1.0 / RG_C)
    rg_lambda = jnp.log(sig) - jnp.log1p(-sig)
    return {
        'x_prompt': rnd((BATCH, SEQ, D_MODEL)),
        'x_sample': rnd((DEC_BATCH, DEC_SEQ, D_MODEL)),
        'mem_prompt': rnd((BATCH, N_MEM, D_MODEL)),
        'state_rg_h': rnd((N_RG_LAYERS, DEC_BATCH, D_RNN), 0.5),
        'state_rg_conv': rnd((N_RG_LAYERS, DEC_BATCH, CONV_W - 1, D_RNN)),
        'cache_ckv': rnd((N_MLA_LAYERS, n_pool, PAGE_SIZE, KV_LORA)),
        'cache_kpe': rnd((N_MLA_LAYERS, n_pool, PAGE_SIZE, QK_ROPE)),
        'cache_mem_k': rnd((DEPTH, DEC_BATCH, N_MEM, MEM_HEADS, MEM_HD)),
        'cache_mem_v': rnd((DEPTH, DEC_BATCH, N_MEM, MEM_HEADS, MEM_HD)),
        'page_table': page_table,
        'norm_mix_pre': gain((DEPTH, D_MODEL)),
        'norm_mix_post': gain((DEPTH, D_MODEL)),
        'norm_ffn_pre': gain((DEPTH, D_MODEL)),
        'norm_ffn_post': gain((DEPTH, D_MODEL)),
        'norm_mem': gain((DEPTH, D_MODEL)),
        'w_mem_kv': w((DEPTH, D_MODEL, 2 * MEM_W), D_MODEL),
        'w_ffn_gate_up': w((DEPTH, D_MODEL, 2 * D_FF), D_MODEL),
        'w_ffn_down': w((DEPTH, D_FF, D_MODEL), D_FF),
        'rg_w_in': w((N_RG_LAYERS, D_MODEL, RG_IN), D_MODEL),
        'rg_conv_w': w((N_RG_LAYERS, CONV_W, D_RNN), CONV_W),
        'rg_conv_b': bias((N_RG_LAYERS, D_RNN)),
        'rg_gate_a_w': w((N_RG_LAYERS, RG_BLOCKS, RG_BLOCK_W, RG_BLOCK_W), RG_BLOCK_W),
        'rg_gate_a_b': bias((N_RG_LAYERS, D_RNN)),
        'rg_gate_x_w': w((N_RG_LAYERS, RG_BLOCKS, RG_BLOCK_W, RG_BLOCK_W), RG_BLOCK_W),
        'rg_gate_x_b': bias((N_RG_LAYERS, D_RNN)),
        'rg_lambda': rg_lambda,
        'rg_w_out': w((N_RG_LAYERS, D_RNN + MEM_W, D_MODEL), D_RNN + MEM_W),
        'mla_w_in': w((N_MLA_LAYERS, D_MODEL, MLA_IN), D_MODEL),
        'mla_q_norm': gain((N_MLA_LAYERS, Q_LORA)),
        'mla_kv_norm': gain((N_MLA_LAYERS, KV_LORA)),
        'mla_w_q_up': w((N_MLA_LAYERS, Q_LORA, MLA_HEADS, QK_NOPE + QK_ROPE), Q_LORA),
        'mla_w_uk': w((N_MLA_LAYERS, KV_LORA, MLA_HEADS, QK_NOPE), KV_LORA),
        'mla_w_uv': w((N_MLA_LAYERS, KV_LORA, MLA_HEADS, V_HEAD), KV_LORA),
        'mla_w_out': w((N_MLA_LAYERS, MLA_HEADS * V_HEAD + MEM_W, D_MODEL), MLA_HEADS * V_HEAD + MEM_W),
    }


def reference(x_prompt, x_sample, mem_prompt, state_rg_h, state_rg_conv, cache_ckv, cache_kpe,
              cache_mem_k, cache_mem_v, page_table,
              norm_mix_pre, norm_mix_post, norm_ffn_pre, norm_ffn_post, norm_mem, w_mem_kv,
              w_ffn_gate_up, w_ffn_down,
              rg_w_in, rg_conv_w, rg_conv_b, rg_gate_a_w, rg_gate_a_b, rg_gate_x_w, rg_gate_x_b,
              rg_lambda, rg_w_out,
              mla_w_in, mla_q_norm, mla_kv_norm, mla_w_q_up, mla_w_uk, mla_w_uv, mla_w_out):
    bp, sp, _ = x_prompt.shape
    bs, ss, _ = x_sample.shape
    past = page_table.shape[1] * PAGE_SIZE
    pos_p = jnp.arange(sp)
    pos_s = past + jnp.arange(ss)

    def attend_prompt(q_lat, q_pe, ckv, kpe):
        b, s, h, c = q_lat.shape
        qb = min(Q_BLOCK, s)
        nqb = s // qb
        qlb = jnp.moveaxis(q_lat.reshape(b, nqb, qb, h, c), 1, 0)
        qpb = jnp.moveaxis(q_pe.reshape(b, nqb, qb, h, QK_ROPE), 1, 0)
        qpos = jnp.arange(s).reshape(nqb, qb)
        kpos = jnp.arange(s)
        per_batch = jax.vmap(mla_attend, in_axes=(0, 0, 0, 0, None, None))
        o = lax.map(lambda a: per_batch(a[0], a[1], ckv, kpe, a[2], kpos), (qlb, qpb, qpos))
        return jnp.moveaxis(o, 0, 1).reshape(b, s, h, c)

    def make_attend_sample(j):
        def attend_sample(q_lat, q_pe, ckv_new, kpe_new):
            sq = q_lat.shape[1]
            kpos = jnp.arange(past + sq)
            qpos = past + jnp.arange(sq)

            def one(a):
                ql, qp, pages, cn, kn = a
                ck = jnp.concatenate([cache_ckv[j, pages].reshape(past, KV_LORA), cn.astype(cache_ckv.dtype)], axis=0)
                kp = jnp.concatenate([cache_kpe[j, pages].reshape(past, QK_ROPE), kn.astype(cache_kpe.dtype)], axis=0)
                return mla_attend(ql, qp, ck, kp, qpos, kpos)

            return lax.map(one, (q_lat, q_pe, page_table, ckv_new, kpe_new))
        return attend_sample

    xp, xs = x_prompt, x_sample
    p_h, p_conv, s_h, s_conv = [], [], [], []
    p_ckv, p_kpe, s_ckv, s_kpe = [], [], [], []
    p_mk, p_mv = [], []
    for l in range(DEPTH):
        j = l // N_MIXERS
        mkv = rms_norm(mem_prompt, norm_mem[l]) @ w_mem_kv[l]
        mk_p = mkv[..., :MEM_W].reshape(bp, N_MEM, MEM_HEADS, MEM_HD)
        mv_p = mkv[..., MEM_W:].reshape(bp, N_MEM, MEM_HEADS, MEM_HD)
        p_mk.append(mk_p)
        p_mv.append(mv_p)
        mk_s, mv_s = cache_mem_k[l], cache_mem_v[l]
        hp = rms_norm(xp, norm_mix_pre[l])
        hs = rms_norm(xs, norm_mix_pre[l])
        if l % N_MIXERS == 0:
            wts = (rg_w_in[j], rg_conv_w[j], rg_conv_b[j], rg_gate_a_w[j], rg_gate_a_b[j],
                   rg_gate_x_w[j], rg_gate_x_b[j], rg_lambda[j], rg_w_out[j])
            op, cp, hlp = rg_sublayer(hp, mk_p, mv_p, jnp.zeros((bp, CONV_W - 1, D_RNN), xp.dtype),
                                      jnp.zeros((bp, D_RNN), xp.dtype), *wts)
            os_, cs, hls = rg_sublayer(hs, mk_s, mv_s, state_rg_conv[j], state_rg_h[j], *wts)
            p_h.append(hlp)
            p_conv.append(cp)
            s_h.append(hls)
            s_conv.append(cs)
        else:
            wts = (mla_w_in[j], mla_q_norm[j], mla_kv_norm[j], mla_w_q_up[j], mla_w_uk[j], mla_w_uv[j], mla_w_out[j])
            op, ckp, kpp = mla_sublayer(hp, mk_p, mv_p, pos_p, attend_prompt, *wts)
            os_, cks, kps = mla_sublayer(hs, mk_s, mv_s, pos_s, make_attend_sample(j), *wts)
            p_ckv.append(ckp)
            p_kpe.append(kpp)
            s_ckv.append(cks)
            s_kpe.append(kps)
        xp = xp + rms_norm(op, norm_mix_post[l])
        xs = xs + rms_norm(os_, norm_mix_post[l])
        xp = xp + rms_norm(swiglu(rms_norm(xp, norm_ffn_pre[l]), w_ffn_gate_up[l], w_ffn_down[l]), norm_ffn_post[l])
        xs = xs + rms_norm(swiglu(rms_norm(xs, norm_ffn_pre[l]), w_ffn_gate_up[l], w_ffn_down[l]), norm_ffn_post[l])

    return (xp, xs,
            jnp.stack(p_h), jnp.stack(p_conv), jnp.stack(p_ckv), jnp.stack(p_kpe),
            jnp.stack(p_mk), jnp.stack(p_mv),
            jnp.stack(s_h), jnp.stack(s_conv), jnp.stack(s_ckv), jnp.stack(s_kpe))
```

```python
import functools

import jax
import jax.numpy as jnp
from jax import lax
from jax.experimental import pallas as pl
from jax.experimental.pallas import tpu as pltpu

F32 = jnp.float32
BF16 = jnp.bfloat16

EPS = 1e-6
RG_C = 8.0
RG_BLOCKS = 8
RG_BLOCK_W = 128
CONV_W = 4
MLA_HEADS = 8
Q_LORA = 512
KV_LORA = 256
QK_NOPE = 128
QK_ROPE = 64
V_HEAD = 128
ROPE_THETA = 10000.0
MLA_SCALE = (QK_NOPE + QK_ROPE) ** -0.5
MEM_HEADS = 4
MEM_HD = 128
MEM_W = MEM_HEADS * MEM_HD
MEM_SCALE = MEM_HD ** -0.5
PAGE_SIZE = 128

LANES = 128
SUBLANES = 8
ROPE_PAD = LANES
KCAT = KV_LORA + ROPE_PAD
VMEM_LIMIT = 56 * 1024 * 1024
NEG_BIG = -1e30
FFN_CHUNK = 1024

NT_DIMS = (((1,), (1,)), ((), ()))
BATCH_NT = (((2,), (2,)), ((0,), (0,)))
BATCH_NN = (((2,), (1,)), ((0,), (0,)))


def _const_spec(shape):
    nd = len(shape)
    return pl.BlockSpec(shape, lambda *_: (0,) * nd, pipeline_mode=pl.Buffered(1))


def _params(n_axes=1):
    return pltpu.CompilerParams(dimension_semantics=("arbitrary",) * n_axes,
                                vmem_limit_bytes=VMEM_LIMIT)


def _rms(x, g):
    return x * lax.rsqrt(jnp.mean(x * x, axis=-1, keepdims=True) + EPS) * g


def _bdot(a, b):
    return jnp.dot(a.astype(BF16), b, preferred_element_type=F32)


def _gelu_tanh(x):
    return 0.5 * x * (1.0 + jnp.tanh(0.7978845608028654 * (x + 0.044715 * (x * x * x))))


def _softplus(x):
    return jnp.maximum(x, 0.0) + jnp.log1p(jnp.exp(-jnp.abs(x)))


def _shared_mem_attend(mq, mk, mv):
    outs = []
    for h in range(MEM_HEADS):
        hs = slice(h * MEM_HD, (h + 1) * MEM_HD)
        q = (mq[:, hs] * MEM_SCALE).astype(BF16)
        s = lax.dot_general(q, mk[:, hs].astype(BF16), NT_DIMS, preferred_element_type=F32)
        p = jnp.exp(s - jnp.max(s, axis=-1, keepdims=True))
        l = jnp.sum(p, axis=-1, keepdims=True)
        outs.append(_bdot(p, mv[:, hs].astype(BF16)) / l)
    return jnp.concatenate(outs, axis=-1)


def _memkv_kernel(mem_ref, g_ref, w_ref, mk_ref, mv_ref):
    kv = _bdot(_rms(mem_ref[...], g_ref[0]), w_ref[0])
    mk_ref[0] = kv[:, :MEM_W]
    mv_ref[0] = kv[:, MEM_W:]


def _memkv(mem, g, w):
    depth = w.shape[0]
    n_mem, d = mem.shape
    out = jax.ShapeDtypeStruct((depth, n_mem, MEM_W), F32)
    return pl.pallas_call(
        _memkv_kernel,
        grid=(depth,),
        in_specs=[_const_spec((n_mem, d)),
                  pl.BlockSpec((1, 1, d), lambda l: (l, 0, 0)),
                  pl.BlockSpec((1, d, 2 * MEM_W), lambda l: (l, 0, 0))],
        out_specs=[pl.BlockSpec((1, n_mem, MEM_W), lambda l: (l, 0, 0))] * 2,
        out_shape=[out, out],
        compiler_params=_params(),
        name="memkv",
    )(mem, g, w)


def _rg_gates(xc, wa_ref, ba, wx_ref, bx, sp, a_out, u_out):
    xcb = xc.astype(BF16)
    for n in range(RG_BLOCKS):
        bs = slice(n * RG_BLOCK_W, (n + 1) * RG_BLOCK_W)
        r = jax.nn.sigmoid(jnp.dot(xcb[:, bs], wa_ref[n], preferred_element_type=F32) + ba[:, bs])
        i = jax.nn.sigmoid(jnp.dot(xcb[:, bs], wx_ref[n], preferred_element_type=F32) + bx[:, bs])
        log_a = -RG_C * r * sp[:, bs]
        a = jnp.exp(log_a)
        a_out[:, bs] = a
        u_out[:, bs] = jnp.sqrt(-jnp.tanh(log_a) * (1.0 + a * a)) * (i * xc[:, bs])


def _rg_prompt_kernel(x_ref, g_ref, win_ref, cw_ref, cb_ref, wa_ref, ba_ref, wx_ref, bx_ref,
                      lam_ref, mk_ref, mv_ref,
                      yrg_ref, ymem_ref, hlast_ref, conv_ref,
                      xb_ref, a_ref, u_ref, h_ref, *, tm):
    d = a_ref.shape[1]
    hist = SUBLANES

    @pl.when(pl.program_id(0) == 0)
    def _():
        xb_ref[0:hist, :] = jnp.zeros((hist, d), F32)
        h_ref[...] = jnp.zeros_like(h_ref)

    z = _bdot(_rms(x_ref[...], g_ref[...]), win_ref[...])
    gate = z[:, :d]
    xb = z[:, d:2 * d]
    mq = z[:, 2 * d:]

    xb_ref[hist:hist + tm, :] = xb
    cw = cw_ref[...]
    xc = xb_ref[hist - 3:hist - 3 + tm, :] * cw[0:1]
    xc = xc + xb_ref[hist - 2:hist - 2 + tm, :] * cw[1:2]
    xc = xc + xb_ref[hist - 1:hist - 1 + tm, :] * cw[2:3]
    xc = xc + xb * cw[3:4]
    xc = xc + cb_ref[...]
    tail = xb_ref[tm + hist - 3:tm + hist, :]
    conv_ref[...] = tail
    xb_ref[hist - 3:hist, :] = tail

    sp = _softplus(-lam_ref[...])
    _rg_gates(xc, wa_ref, ba_ref[...], wx_ref, bx_ref[...], sp, a_ref, u_ref)

    row = lax.broadcasted_iota(jnp.int32, (SUBLANES, d), 0)

    def group(g, h):
        r0 = pl.multiple_of(g * SUBLANES, SUBLANES)
        a = a_ref[pl.ds(r0, SUBLANES), :]
        b = u_ref[pl.ds(r0, SUBLANES), :]
        for s in (1, 2, 4):
            keep = row >= s
            b = jnp.where(keep, a * pltpu.roll(b, s, axis=0) + b, b)
            a = jnp.where(keep, a * pltpu.roll(a, s, axis=0), a)
        hs = a * h + b
        u_ref[pl.ds(r0, SUBLANES), :] = hs
        return hs[SUBLANES - 1:SUBLANES, :]

    h_end = lax.fori_loop(0, tm // SUBLANES, group, h_ref[...])
    h_ref[...] = h_end
    hlast_ref[...] = h_end

    yrg_ref[...] = (u_ref[...] * _gelu_tanh(gate)).astype(BF16)
    ymem_ref[...] = _shared_mem_attend(mq, mk_ref[...], mv_ref[...]).astype(BF16)


def _rg_prompt(x, g, w_in, cw, cb, wa, ba, wx, bx, lam, mk, mv, *, tm):
    s, d = x.shape
    rg_in = w_in.shape[1]
    row = lambda i: (i, 0)
    return pl.pallas_call(
        functools.partial(_rg_prompt_kernel, tm=tm),
        grid=(s // tm,),
        in_specs=[pl.BlockSpec((tm, d), row),
                  _const_spec((1, d)), _const_spec((d, rg_in)),
                  _const_spec((CONV_W, d)), _const_spec((1, d)),
                  _const_spec(wa.shape), _const_spec((1, d)),
                  _const_spec(wx.shape), _const_spec((1, d)),
                  _const_spec((1, d)),
                  _const_spec(mk.shape), _const_spec(mv.shape)],
        out_specs=[pl.BlockSpec((tm, d), row),
                   pl.BlockSpec((tm, MEM_W), row),
                   pl.BlockSpec((1, d), lambda i: (0, 0)),
                   pl.BlockSpec((CONV_W - 1, d), lambda i: (0, 0))],
        out_shape=[jax.ShapeDtypeStruct((s, d), BF16),
                   jax.ShapeDtypeStruct((s, MEM_W), BF16),
                   jax.ShapeDtypeStruct((1, d), F32),
                   jax.ShapeDtypeStruct((CONV_W - 1, d), F32)],
        scratch_shapes=[pltpu.VMEM((tm + SUBLANES, d), F32),
                        pltpu.VMEM((tm, d), F32),
                        pltpu.VMEM((tm, d), F32),
                        pltpu.VMEM((1, d), F32)],
        compiler_params=_params(),
        name="rg_prompt",
    )(x, g, w_in, cw, cb, wa, ba, wx, bx, lam, mk, mv)


def _rg_sample_kernel(x_ref, g_ref, win_ref, cw_ref, cb_ref, wa_ref, ba_ref, wx_ref, bx_ref,
                      lam_ref, prev_ref, h0_ref,
                      yrg_ref, mq_ref, hnew_ref, conv_ref, a_ref, u_ref):
    d = a_ref.shape[1]
    z = _bdot(_rms(x_ref[...], g_ref[...]), win_ref[...])
    gate = z[:, :d]
    xb = z[:, d:2 * d]
    mq_ref[...] = z[:, 2 * d:]

    cw = cw_ref[...]
    xc = prev_ref[0] * cw[0:1]
    xc = xc + prev_ref[1] * cw[1:2]
    xc = xc + prev_ref[2] * cw[2:3]
    xc = xc + xb * cw[3:4]
    xc = xc + cb_ref[...]
    conv_ref[0] = prev_ref[1]
    conv_ref[1] = prev_ref[2]
    conv_ref[2] = xb

    sp = _softplus(-lam_ref[...])
    _rg_gates(xc, wa_ref, ba_ref[...], wx_ref, bx_ref[...], sp, a_ref, u_ref)
    h = u_ref[...] + a_ref[...] * h0_ref[...]
    hnew_ref[...] = h
    yrg_ref[...] = (h * _gelu_tanh(gate)).astype(BF16)


def _rg_sample(x, g, w_in, cw, cb, wa, ba, wx, bx, lam, prev, h0):
    b, d = x.shape
    args = (x, g, w_in, cw, cb, wa, ba, wx, bx, lam, prev, h0)
    return pl.pallas_call(
        _rg_sample_kernel,
        grid=(1,),
        in_specs=[_const_spec(a.shape) for a in args],
        out_specs=[_const_spec((b, d)), _const_spec((b, MEM_W)), _const_spec((b, d)),
                   _const_spec((CONV_W - 1, b, d))],
        out_shape=[jax.ShapeDtypeStruct((b, d), BF16),
                   jax.ShapeDtypeStruct((b, MEM_W), F32),
                   jax.ShapeDtypeStruct((b, d), F32),
                   jax.ShapeDtypeStruct((CONV_W - 1, b, d), F32)],
        scratch_shapes=[pltpu.VMEM((b, d), F32), pltpu.VMEM((b, d), F32)],
        compiler_params=_params(),
        name="rg_sample",
    )(*args)


def _mem_sample_kernel(q_ref, k_ref, v_ref, o_ref):
    g = q_ref.shape[0]
    for h in range(MEM_HEADS):
        hs = slice(h * MEM_HD, (h + 1) * MEM_HD)
        q = jnp.broadcast_to(q_ref[:, :, hs] * MEM_SCALE, (g, 2 * SUBLANES, MEM_HD)).astype(BF16)
        s = lax.dot_general(q, k_ref[:, :, hs].astype(BF16), BATCH_NT, preferred_element_type=F32)
        p = jnp.exp(s - jnp.max(s, axis=-1, keepdims=True))
        l = jnp.sum(p, axis=-1, keepdims=True)
        o = lax.dot_general(p.astype(BF16), v_ref[:, :, hs].astype(BF16), BATCH_NN,
                            preferred_element_type=F32) / l
        o_ref[:, :, hs] = o[:, 0:1, :]


def _mem_sample(mq, k, v, *, group):
    b, n_mem, w = k.shape
    out = pl.pallas_call(
        _mem_sample_kernel,
        grid=(b // group,),
        in_specs=[pl.BlockSpec((group, 1, w), lambda i: (i, 0, 0)),
                  pl.BlockSpec((group, n_mem, w), lambda i: (i, 0, 0)),
                  pl.BlockSpec((group, n_mem, w), lambda i: (i, 0, 0))],
        out_specs=pl.BlockSpec((group, 1, w), lambda i: (i, 0, 0)),
        out_shape=jax.ShapeDtypeStruct((b, 1, w), F32),
        compiler_params=_params(),
        name="mem_sample",
    )(mq.reshape(b, 1, w), k, v)
    return out.reshape(b, w)


def _mla_pre_kernel(x_ref, g_ref, win_ref, qn_ref, kvn_ref, wqn_ref, wqp_ref, wqs_ref, wuk_ref,
                    inv_ref, sign_ref, *rest, tm, pos_base, pos_stride, shared_mem):
    if shared_mem:
        mk_ref, mv_ref, ckv_ref, kpe_ref, kcat_ref, qcat_ref, mem_ref = rest
    else:
        ckv_ref, kpe_ref, kcat_ref, qcat_ref, mem_ref = rest
    o_kv = Q_LORA
    o_mq = Q_LORA + KV_LORA
    o_pe = o_mq + MEM_W
    o_ps = o_pe + ROPE_PAD

    z = _bdot(_rms(x_ref[...], g_ref[...]), win_ref[...])
    cq = _rms(z[:, :o_kv], qn_ref[...])
    ckv = _rms(z[:, o_kv:o_mq], kvn_ref[...])
    mq = z[:, o_mq:o_pe]

    t = pl.program_id(0) * tm + lax.broadcasted_iota(jnp.int32, (tm, ROPE_PAD), 0)
    pos = (pos_base + pos_stride * t).astype(F32)
    ang = pos * inv_ref[...]
    cos = jnp.cos(ang)
    sin = jnp.sin(ang) * sign_ref[...]

    kpe = z[:, o_pe:o_ps] * cos + z[:, o_ps:] * sin
    ckv_ref[...] = ckv
    kpe_ref[...] = kpe[:, :QK_ROPE]
    kcat_ref[...] = jnp.concatenate([ckv, kpe], axis=-1).astype(BF16)

    cqb = cq.astype(BF16)
    q_nope = jnp.dot(cqb, wqn_ref[...], preferred_element_type=F32)
    q_rope = jnp.dot(cqb, wqp_ref[...], preferred_element_type=F32)
    q_swap = jnp.dot(cqb, wqs_ref[...], preferred_element_type=F32)
    for h in range(MLA_HEADS):
        ns = slice(h * QK_NOPE, (h + 1) * QK_NOPE)
        ps = slice(h * ROPE_PAD, (h + 1) * ROPE_PAD)
        q_lat = _bdot(q_nope[:, ns], wuk_ref[h])
        q_pe = q_rope[:, ps] * cos + q_swap[:, ps] * sin
        qcat_ref[h] = (jnp.concatenate([q_lat, q_pe], axis=-1) * MLA_SCALE).astype(BF16)

    if shared_mem:
        mem_ref[...] = _shared_mem_attend(mq, mk_ref[...], mv_ref[...]).astype(mem_ref.dtype)
    else:
        mem_ref[...] = mq


def _mla_pre(x, g, w_in, qn, kvn, wqn, wqp, wqs, wuk, inv, sign, mem_kv, *, tm, pos_base, pos_stride):
    s, d = x.shape
    shared_mem = mem_kv is not None
    row = lambda i: (i, 0)
    consts = [g, w_in, qn, kvn, wqn, wqp, wqs, wuk, inv, sign] + (list(mem_kv) if shared_mem else [])
    return pl.pallas_call(
        functools.partial(_mla_pre_kernel, tm=tm, pos_base=pos_base, pos_stride=pos_stride,
                          shared_mem=shared_mem),
        grid=(s // tm,),
        in_specs=[pl.BlockSpec((tm, d), row)] + [_const_spec(a.shape) for a in consts],
        out_specs=[pl.BlockSpec((tm, KV_LORA), row),
                   pl.BlockSpec((tm, QK_ROPE), row),
                   pl.BlockSpec((tm, KCAT), row),
                   pl.BlockSpec((MLA_HEADS, tm, KCAT), lambda i: (0, i, 0)),
                   pl.BlockSpec((tm, MEM_W), row)],
        out_shape=[jax.ShapeDtypeStruct((s, KV_LORA), F32),
                   jax.ShapeDtypeStruct((s, QK_ROPE), F32),
                   jax.ShapeDtypeStruct((s, KCAT), BF16),
                   jax.ShapeDtypeStruct((MLA_HEADS, s, KCAT), BF16),
                   jax.ShapeDtypeStruct((s, MEM_W), BF16 if shared_mem else F32)],
        compiler_params=_params(),
        name="mla_pre",
    )(x, *consts)


def _mla_prompt_attn_kernel(q_ref, k_ref, wuv_ref, y_ref, m_ref, l_ref, acc_ref, *, tq, tk):
    i = pl.program_id(0)
    rows = MLA_HEADS * tq
    m_ref[...] = jnp.full_like(m_ref, NEG_BIG)
    l_ref[...] = jnp.zeros_like(l_ref)
    acc_ref[...] = jnp.zeros_like(acc_ref)

    def step(k_start, masked):
        q = q_ref[...].reshape(rows, KCAT)
        kt = k_ref[pl.ds(k_start, tk), :]
        s = lax.dot_general(q, kt, NT_DIMS, preferred_element_type=F32)
        if masked:
            q_pos = i * tq + (lax.broadcasted_iota(jnp.int32, (rows, tk), 0) & (tq - 1))
            k_pos = k_start + lax.broadcasted_iota(jnp.int32, (rows, tk), 1)
            s = jnp.where(k_pos <= q_pos, s, NEG_BIG)
        m_prev = m_ref[...]
        m_new = jnp.maximum(m_prev, jnp.max(s, axis=-1, keepdims=True))
        alpha = jnp.exp(m_prev - m_new)
        p = jnp.exp(s - m_new)
        l_ref[...] = alpha * l_ref[...] + jnp.sum(p, axis=-1, keepdims=True)
        acc_ref[...] = alpha * acc_ref[...] + _bdot(p, kt[:, :KV_LORA])
        m_ref[...] = m_new

    n_full = (i * tq) // tk

    def body(j, c):
        step(pl.multiple_of(j * tk, tk), False)
        return c

    lax.fori_loop(0, n_full, body, 0)
    step(pl.multiple_of(n_full * tk, tk), True)

    o = acc_ref[...] / l_ref[...]
    for h in range(MLA_HEADS):
        y_ref[:, h * V_HEAD:(h + 1) * V_HEAD] = _bdot(o[h * tq:(h + 1) * tq], wuv_ref[h]).astype(y_ref.dtype)


def _mla_prompt_attn(qcat, kcat, wuv, *, tq, tk):
    heads, s, _ = qcat.shape
    rows = heads * tq
    return pl.pallas_call(
        functools.partial(_mla_prompt_attn_kernel, tq=tq, tk=tk),
        grid=(s // tq,),
        in_specs=[pl.BlockSpec((heads, tq, KCAT), lambda i: (0, i, 0)),
                  _const_spec(kcat.shape), _const_spec(wuv.shape)],
        out_specs=pl.BlockSpec((tq, heads * V_HEAD), lambda i: (i, 0)),
        out_shape=jax.ShapeDtypeStruct((s, heads * V_HEAD), BF16),
        scratch_shapes=[pltpu.VMEM((rows, 1), F32), pltpu.VMEM((rows, 1), F32),
                        pltpu.VMEM((rows, KV_LORA), F32)],
        compiler_params=_params(),
        name="mla_prompt_attn",
    )(qcat, kcat, wuv)


def _mla_sample_attn_kernel(pt_ref, q_ref, knew_ref, ckv_hbm, kpe_hbm, o_ref,
                            ckv_buf, kpe_buf, sems, m_ref, l_ref, acc_ref, *, chunk_pages):
    b = pl.program_id(0)
    c = pl.program_id(1)
    n_b = pl.num_programs(0)
    n_c = pl.num_programs(1)
    step = b * n_c + c
    slot = step % 2

    def copies(bb, cc, sl):
        out = []
        for p in range(chunk_pages):
            page = pt_ref[bb, cc * chunk_pages + p]
            rows = pl.ds(p * PAGE_SIZE, PAGE_SIZE)
            out.append(pltpu.make_async_copy(ckv_hbm.at[page], ckv_buf.at[sl, rows], sems.at[0, sl]))
            out.append(pltpu.make_async_copy(kpe_hbm.at[page], kpe_buf.at[sl, rows], sems.at[1, sl]))
        return out

    @pl.when(step == 0)
    def _():
        for cp in copies(b, c, slot):
            cp.start()

    @pl.when(step + 1 < n_b * n_c)
    def _():
        nxt = step + 1
        for cp in copies(nxt // n_c, nxt % n_c, 1 - slot):
            cp.start()

    @pl.when(c == 0)
    def _():
        m_ref[...] = jnp.full_like(m_ref, NEG_BIG)
        l_ref[...] = jnp.zeros_like(l_ref)
        acc_ref[...] = jnp.zeros_like(acc_ref)

    for cp in copies(b, c, slot):
        cp.wait()

    q = q_ref[0]
    kc = ckv_buf[slot].astype(BF16)
    kp = kpe_buf[slot].astype(BF16)
    s = lax.dot_general(q[:, :KV_LORA], kc, NT_DIMS, preferred_element_type=F32)
    s = s + lax.dot_general(q[:, KV_LORA:KV_LORA + QK_ROPE], kp, NT_DIMS, preferred_element_type=F32)
    m_prev = m_ref[...]
    m_new = jnp.maximum(m_prev, jnp.max(s, axis=-1, keepdims=True))
    alpha = jnp.exp(m_prev - m_new)
    p = jnp.exp(s - m_new)
    l_ref[...] = alpha * l_ref[...] + jnp.sum(p, axis=-1, keepdims=True)
    acc_ref[...] = alpha * acc_ref[...] + _bdot(p, kc)
    m_ref[...] = m_new

    @pl.when(c == n_c - 1)
    def _():
        kn = knew_ref[0].astype(F32)
        s_new = jnp.sum(q.astype(F32) * kn, axis=-1, keepdims=True)
        m_prev = m_ref[...]
        m_new = jnp.maximum(m_prev, s_new)
        alpha = jnp.exp(m_prev - m_new)
        p_new = jnp.exp(s_new - m_new)
        l = alpha * l_ref[...] + p_new
        acc = alpha * acc_ref[...] + p_new.astype(BF16).astype(F32) * kn[:, :KV_LORA]
        o_ref[0] = acc / l


def _mla_sample_attn(page_table, qcat, knew, cache_ckv, cache_kpe, *, chunk_pages):
    b, heads, _ = qcat.shape
    n_pages = page_table.shape[1]
    n_c = n_pages // chunk_pages
    chunk = chunk_pages * PAGE_SIZE
    grid_spec = pltpu.PrefetchScalarGridSpec(
        num_scalar_prefetch=1,
        grid=(b, n_c),
        in_specs=[pl.BlockSpec((1, heads, KCAT), lambda i, c, pt: (i, 0, 0)),
                  pl.BlockSpec((1, 1, KCAT), lambda i, c, pt: (i, 0, 0)),
                  pl.BlockSpec(memory_space=pl.ANY),
                  pl.BlockSpec(memory_space=pl.ANY)],
        out_specs=pl.BlockSpec((1, heads, KV_LORA), lambda i, c, pt: (i, 0, 0)),
        scratch_shapes=[pltpu.VMEM((2, chunk, KV_LORA), F32),
                        pltpu.VMEM((2, chunk, QK_ROPE), F32),
                        pltpu.SemaphoreType.DMA((2, 2)),
                        pltpu.VMEM((heads, 1), F32), pltpu.VMEM((heads, 1), F32),
                        pltpu.VMEM((heads, KV_LORA), F32)],
    )
    return pl.pallas_call(
        functools.partial(_mla_sample_attn_kernel, chunk_pages=chunk_pages),
        grid_spec=grid_spec,
        out_shape=jax.ShapeDtypeStruct((b, heads, KV_LORA), F32),
        compiler_params=_params(2),
        name="mla_sample_attn",
    )(page_table, qcat, knew.reshape(b, 1, KCAT), cache_ckv, cache_kpe)


def _uv_kernel(o_ref, wuv_ref, y_ref):
    for h in range(MLA_HEADS):
        y_ref[:, h * V_HEAD:(h + 1) * V_HEAD] = _bdot(o_ref[h], wuv_ref[h]).astype(y_ref.dtype)


def _uv_project(o_hm, wuv):
    heads, b, _ = o_hm.shape
    return pl.pallas_call(
        _uv_kernel,
        grid=(1,),
        in_specs=[_const_spec(o_hm.shape), _const_spec(wuv.shape)],
        out_specs=_const_spec((b, heads * V_HEAD)),
        out_shape=jax.ShapeDtypeStruct((b, heads * V_HEAD), BF16),
        compiler_params=_params(),
        name="uv_project",
    )(o_hm, wuv)


def _out_ffn_kernel(x_ref, ya_ref, ym_ref, wo_ref, gpost_ref, gpre_ref, wgu_ref, wd_ref, gfpost_ref, o_ref):
    da = ya_ref.shape[1]
    dff = wd_ref.shape[0]
    mix = _bdot(ya_ref[...], wo_ref[:da, :]) + _bdot(ym_ref[...], wo_ref[da:, :])
    x1 = x_ref[...] + _rms(mix, gpost_ref[...])
    hb = _rms(x1, gpre_ref[...]).astype(BF16)
    ffn = None
    for c0 in range(0, dff, FFN_CHUNK):
        c1 = min(c0 + FFN_CHUNK, dff)
        gate = jnp.dot(hb, wgu_ref[:, c0:c1], preferred_element_type=F32)
        up = jnp.dot(hb, wgu_ref[:, dff + c0:dff + c1], preferred_element_type=F32)
        part = _bdot((gate * jax.nn.sigmoid(gate)) * up, wd_ref[c0:c1, :])
        ffn = part if ffn is None else ffn + part
    o_ref[...] = x1 + _rms(ffn, gfpost_ref[...])


def _out_ffn(x, ya, ym, wo, gpost, gpre, wgu, wd, gfpost, *, tm):
    s, d = x.shape
    row = lambda i: (i, 0)
    consts = (wo, gpost, gpre, wgu, wd, gfpost)
    return pl.pallas_call(
        _out_ffn_kernel,
        grid=(s // tm,),
        in_specs=[pl.BlockSpec((tm, d), row),
                  pl.BlockSpec((tm, ya.shape[1]), row),
                  pl.BlockSpec((tm, ym.shape[1]), row)] + [_const_spec(a.shape) for a in consts],
        out_specs=pl.BlockSpec((tm, d), row),
        out_shape=jax.ShapeDtypeStruct((s, d), F32),
        compiler_params=_params(),
        name="out_ffn",
    )(x, ya, ym, *consts)


def _pad_cols(w, width):
    return jnp.pad(w, ((0, 0), (0, width - w.shape[1])))


def _swap_halves(w):
    half = w.shape[-1] // 2
    return jnp.concatenate([w[..., half:], w[..., :half]], axis=-1)


def _mla_weights(w_in, w_q_up, w_uk, w_uv):
    o2, o3 = Q_LORA + KV_LORA, Q_LORA + KV_LORA + QK_ROPE
    w_pe = w_in[:, o2:o3]
    w_in_ext = jnp.concatenate(
        [w_in[:, :o2], w_in[:, o3:], _pad_cols(w_pe, ROPE_PAD), _pad_cols(_swap_halves(w_pe), ROPE_PAD)],
        axis=1).astype(BF16)
    wq_nope = w_q_up[:, :, :QK_NOPE].reshape(Q_LORA, MLA_HEADS * QK_NOPE).astype(BF16)
    wq_pe = w_q_up[:, :, QK_NOPE:]
    pad = ((0, 0), (0, 0), (0, ROPE_PAD - QK_ROPE))
    wq_rope = jnp.pad(wq_pe, pad).reshape(Q_LORA, MLA_HEADS * ROPE_PAD).astype(BF16)
    wq_swap = jnp.pad(_swap_halves(wq_pe), pad).reshape(Q_LORA, MLA_HEADS * ROPE_PAD).astype(BF16)
    wuk_t = jnp.transpose(w_uk, (1, 2, 0)).astype(BF16)
    wuv_t = jnp.transpose(w_uv, (1, 0, 2)).astype(BF16)
    return w_in_ext, wq_nope, wq_rope, wq_swap, wuk_t, wuv_t


def kernel(x_prompt, x_sample, mem_prompt, state_rg_h, state_rg_conv, cache_ckv, cache_kpe, cache_mem_k, cache_mem_v, page_table, norm_mix_pre, norm_mix_post, norm_ffn_pre, norm_ffn_post, norm_mem, w_mem_kv, w_ffn_gate_up, w_ffn_down, rg_w_in, rg_conv_w, rg_conv_b, rg_gate_a_w, rg_gate_a_b, rg_gate_x_w, rg_gate_x_b, rg_lambda, rg_w_out, mla_w_in, mla_q_norm, mla_kv_norm, mla_w_q_up, mla_w_uk, mla_w_uv, mla_w_out):
    bp, sp, d = x_prompt.shape
    bs, ss, _ = x_sample.shape
    depth = norm_mix_pre.shape[0]
    n_mem = mem_prompt.shape[1]
    assert bp == 1 and ss == 1 and depth == 2
    past = page_table.shape[1] * PAGE_SIZE
    tm = 512

    row = lambda a, l: a[l].reshape(1, -1)
    xp = x_prompt.reshape(sp, d)
    xs = x_sample.reshape(bs, d)

    mk_p, mv_p = _memkv(mem_prompt.reshape(n_mem, d), norm_mem.reshape(depth, 1, d), w_mem_kv.astype(BF16))

    half = jnp.arange(0, QK_ROPE, 2, dtype=F32) / QK_ROPE
    inv = ROPE_THETA ** (-half)
    zeros = jnp.zeros((ROPE_PAD - QK_ROPE,), F32)
    inv_pad = jnp.concatenate([inv, inv, zeros]).reshape(1, ROPE_PAD)
    sign_pad = jnp.concatenate([-jnp.ones_like(inv), jnp.ones_like(inv), zeros]).reshape(1, ROPE_PAD)

    def ffn_args(l):
        return (row(norm_mix_post, l), row(norm_ffn_pre, l), w_ffn_gate_up[l].astype(BF16),
                w_ffn_down[l].astype(BF16), row(norm_ffn_post, l))

    rg = (row(norm_mix_pre, 0), rg_w_in[0].astype(BF16), rg_conv_w[0], row(rg_conv_b, 0),
          rg_gate_a_w[0].astype(BF16), row(rg_gate_a_b, 0), rg_gate_x_w[0].astype(BF16), row(rg_gate_x_b, 0),
          row(rg_lambda, 0))
    wo0 = rg_w_out[0].astype(BF16)
    yrg_p, ymem_p, p_h, p_conv = _rg_prompt(xp, *rg, mk_p[0], mv_p[0], tm=tm)
    post0, pre0, wgu0, wd0, fpost0 = ffn_args(0)
    xp = _out_ffn(xp, yrg_p, ymem_p, wo0, post0, pre0, wgu0, wd0, fpost0, tm=tm)

    prev_s = jnp.transpose(state_rg_conv[0], (1, 0, 2))
    yrg_s, mq_s, s_h, s_conv = _rg_sample(xs, *rg, prev_s, state_rg_h[0])
    ymem_s = _mem_sample(mq_s, cache_mem_k[0].reshape(bs, n_mem, MEM_W),
                         cache_mem_v[0].reshape(bs, n_mem, MEM_W), group=8)
    xs = _out_ffn(xs, yrg_s, ymem_s, wo0, post0, pre0, wgu0, wd0, fpost0, tm=bs)

    w_in_ext, wq_nope, wq_rope, wq_swap, wuk_t, wuv_t = _mla_weights(mla_w_in[0], mla_w_q_up[0], mla_w_uk[0], mla_w_uv[0])
    mla = (row(norm_mix_pre, 1), w_in_ext, row(mla_q_norm, 0), row(mla_kv_norm, 0),
           wq_nope, wq_rope, wq_swap, wuk_t, inv_pad, sign_pad)
    wo1 = mla_w_out[0].astype(BF16)
    post1, pre1, wgu1, wd1, fpost1 = ffn_args(1)

    p_ckv, p_kpe, kcat_p, qcat_p, ymem_p = _mla_pre(xp, *mla, (mk_p[1], mv_p[1]), tm=tm, pos_base=0, pos_stride=1)
    yatt_p = _mla_prompt_attn(qcat_p, kcat_p, wuv_t, tq=128, tk=512)
    xp = _out_ffn(xp, yatt_p, ymem_p, wo1, post1, pre1, wgu1, wd1, fpost1, tm=tm)

    s_ckv, s_kpe, kcat_s, qcat_s, mq_s = _mla_pre(xs, *mla, None, tm=bs, pos_base=past, pos_stride=0)
    ymem_s = _mem_sample(mq_s, cache_mem_k[1].reshape(bs, n_mem, MEM_W),
                         cache_mem_v[1].reshape(bs, n_mem, MEM_W), group=8)
    o_lat = _mla_sample_attn(page_table, jnp.transpose(qcat_s, (1, 0, 2)), kcat_s, cache_ckv[0], cache_kpe[0],
                             chunk_pages=32)
    yatt_s = _uv_project(jnp.transpose(o_lat, (1, 0, 2)).astype(BF16), wuv_t)
    xs = _out_ffn(xs, yatt_s, ymem_s, wo1, post1, pre1, wgu1, wd1, fpost1, tm=bs)

    mem_shape = (depth, bp, n_mem, MEM_HEADS, MEM_HD)
    return (xp.reshape(bp, sp, d), xs.reshape(bs, ss, d),
            p_h.reshape(1, bp, d), p_conv.reshape(1, bp, CONV_W - 1, d),
            p_ckv.reshape(1, bp, sp, KV_LORA), p_kpe.reshape(1, bp, sp, QK_ROPE),
            mk_p.reshape(mem_shape), mv_p.reshape(mem_shape),
            s_h.reshape(1, bs, d), jnp.transpose(s_conv, (1, 0, 2)).reshape(1, bs, CONV_W - 1, d),
            s_ckv.reshape(1, bs, ss, KV_LORA), s_kpe.reshape(1, bs, ss, QK_ROPE))
```

```python
import functools

import jax
import jax.numpy as jnp
from jax import lax
from jax.experimental import pallas as pl
from jax.experimental.pallas import tpu as pltpu

F32 = jnp.float32
BF16 = jnp.bfloat16

EPS = 1e-6
RG_C = 8.0
RG_BLOCKS = 8
RG_BLOCK_W = 128
CONV_W = 4
MLA_HEADS = 8
Q_LORA = 512
KV_LORA = 256
QK_NOPE = 128
QK_ROPE = 64
V_HEAD = 128
ROPE_THETA = 10000.0
MLA_SCALE = (QK_NOPE + QK_ROPE) ** -0.5
MEM_HEADS = 4
MEM_HD = 128
MEM_W = MEM_HEADS * MEM_HD
MEM_SCALE = MEM_HD ** -0.5
PAGE_SIZE = 128

LANES = 128
SUBLANES = 8
ROPE_PAD = LANES
KCAT = KV_LORA + ROPE_PAD
VMEM_LIMIT = 56 * 1024 * 1024
NEG_BIG = -1e30
FFN_CHUNK = 1024

NT_DIMS = (((1,), (1,)), ((), ()))
BATCH_NT = (((2,), (2,)), ((0,), (0,)))
BATCH_NN = (((2,), (1,)), ((0,), (0,)))


def _const_spec(shape):
    nd = len(shape)
    return pl.BlockSpec(shape, lambda *_: (0,) * nd, pipeline_mode=pl.Buffered(1))


def _params(n_axes=1):
    return pltpu.CompilerParams(dimension_semantics=("arbitrary",) * n_axes,
                                vmem_limit_bytes=VMEM_LIMIT)


def _rms(x, g):
    return x * lax.rsqrt(jnp.mean(x * x, axis=-1, keepdims=True) + EPS) * g


def _bdot(a, b):
    return jnp.dot(a.astype(BF16), b, preferred_element_type=F32)


def _gelu_tanh(x):
    return 0.5 * x * (1.0 + jnp.tanh(0.7978845608028654 * (x + 0.044715 * (x * x * x))))


def _softplus(x):
    return jnp.maximum(x, 0.0) + jnp.log1p(jnp.exp(-jnp.abs(x)))


def _shared_mem_attend(mq, mk, mv):
    outs = []
    for h in range(MEM_HEADS):
        hs = slice(h * MEM_HD, (h + 1) * MEM_HD)
        q = (mq[:, hs] * MEM_SCALE).astype(BF16)
        s = lax.dot_general(q, mk[:, hs].astype(BF16), NT_DIMS, preferred_element_type=F32)
        p = jnp.exp(s - jnp.max(s, axis=-1, keepdims=True))
        l = jnp.sum(p, axis=-1, keepdims=True)
        outs.append(_bdot(p, mv[:, hs].astype(BF16)) / l)
    return jnp.concatenate(outs, axis=-1)


def _memkv_kernel(mem_ref, g_ref, w_ref, mk_ref, mv_ref):
    kv = _bdot(_rms(mem_ref[...], g_ref[0]), w_ref[0])
    mk_ref[0] = kv[:, :MEM_W]
    mv_ref[0] = kv[:, MEM_W:]


def _memkv(mem, g, w):
    depth = w.shape[0]
    n_mem, d = mem.shape
    out = jax.ShapeDtypeStruct((depth, n_mem, MEM_W), F32)
    return pl.pallas_call(
        _memkv_kernel,
        grid=(depth,),
        in_specs=[_const_spec((n_mem, d)),
                  pl.BlockSpec((1, 1, d), lambda l: (l, 0, 0)),
                  pl.BlockSpec((1, d, 2 * MEM_W), lambda l: (l, 0, 0))],
        out_specs=[pl.BlockSpec((1, n_mem, MEM_W), lambda l: (l, 0, 0))] * 2,
        out_shape=[out, out],
        compiler_params=_params(),
        name="memkv",
    )(mem, g, w)


def _rg_gates(xc, wa_ref, ba, wx_ref, bx, sp, a_out, u_out):
    xcb = xc.astype(BF16)
    for n in range(RG_BLOCKS):
        bs = slice(n * RG_BLOCK_W, (n + 1) * RG_BLOCK_W)
        r = jax.nn.sigmoid(jnp.dot(xcb[:, bs], wa_ref[n], preferred_element_type=F32) + ba[:, bs])
        i = jax.nn.sigmoid(jnp.dot(xcb[:, bs], wx_ref[n], preferred_element_type=F32) + bx[:, bs])
        log_a = -RG_C * r * sp[:, bs]
        a = jnp.exp(log_a)
        a_out[:, bs] = a
        u_out[:, bs] = jnp.sqrt(-jnp.tanh(log_a) * (1.0 + a * a)) * (i * xc[:, bs])


def _rg_prompt_kernel(x_ref, g_ref, win_ref, cw_ref, cb_ref, wa_ref, ba_ref, wx_ref, bx_ref,
                      lam_ref, mk_ref, mv_ref,
                      yrg_ref, ymem_ref, hlast_ref, conv_ref,
                      xb_ref, a_ref, u_ref, h_ref, *, tm):
    d = a_ref.shape[1]
    hist = SUBLANES

    @pl.when(pl.program_id(0) == 0)
    def _():
        xb_ref[0:hist, :] = jnp.zeros((hist, d), F32)
        h_ref[...] = jnp.zeros_like(h_ref)

    z = _bdot(_rms(x_ref[...], g_ref[...]), win_ref[...])
    gate = z[:, :d]
    xb = z[:, d:2 * d]
    mq = z[:, 2 * d:]

    xb_ref[hist:hist + tm, :] = xb
    cw = cw_ref[...]
    xc = xb_ref[hist - 3:hist - 3 + tm, :] * cw[0:1]
    xc = xc + xb_ref[hist - 2:hist - 2 + tm, :] * cw[1:2]
    xc = xc + xb_ref[hist - 1:hist - 1 + tm, :] * cw[2:3]
    xc = xc + xb * cw[3:4]
    xc = xc + cb_ref[...]
    tail = xb_ref[tm + hist - 3:tm + hist, :]
    conv_ref[...] = tail
    xb_ref[hist - 3:hist, :] = tail

    sp = _softplus(-lam_ref[...])
    _rg_gates(xc, wa_ref, ba_ref[...], wx_ref, bx_ref[...], sp, a_ref, u_ref)

    row = lax.broadcasted_iota(jnp.int32, (SUBLANES, d), 0)

    def group(g, h):
        r0 = pl.multiple_of(g * SUBLANES, SUBLANES)
        a = a_ref[pl.ds(r0, SUBLANES), :]
        b = u_ref[pl.ds(r0, SUBLANES), :]
        for s in (1, 2, 4):
            keep = row >= s
            b = jnp.where(keep, a * pltpu.roll(b, s, axis=0) + b, b)
            a = jnp.where(keep, a * pltpu.roll(a, s, axis=0), a)
        hs = a * h + b
        u_ref[pl.ds(r0, SUBLANES), :] = hs
        return hs[SUBLANES - 1:SUBLANES, :]

    h_end = lax.fori_loop(0, tm // SUBLANES, group, h_ref[...])
    h_ref[...] = h_end
    hlast_ref[...] = h_end

    yrg_ref[...] = (u_ref[...] * _gelu_tanh(gate)).astype(BF16)
    ymem_ref[...] = _shared_mem_attend(mq, mk_ref[...], mv_ref[...]).astype(BF16)


def _rg_prompt(x, g, w_in, cw, cb, wa, ba, wx, bx, lam, mk, mv, *, tm):
    s, d = x.shape
    rg_in = w_in.shape[1]
    row = lambda i: (i, 0)
    return pl.pallas_call(
        functools.partial(_rg_prompt_kernel, tm=tm),
        grid=(s // tm,),
        in_specs=[pl.BlockSpec((tm, d), row),
                  _const_spec((1, d)), _const_spec((d, rg_in)),
                  _const_spec((CONV_W, d)), _const_spec((1, d)),
                  _const_spec(wa.shape), _const_spec((1, d)),
                  _const_spec(wx.shape), _const_spec((1, d)),
                  _const_spec((1, d)),
                  _const_spec(mk.shape), _const_spec(mv.shape)],
        out_specs=[pl.BlockSpec((tm, d), row),
                   pl.BlockSpec((tm, MEM_W), row),
                   pl.BlockSpec((1, d), lambda i: (0, 0)),
                   pl.BlockSpec((CONV_W - 1, d), lambda i: (0, 0))],
        out_shape=[jax.ShapeDtypeStruct((s, d), BF16),
                   jax.ShapeDtypeStruct((s, MEM_W), BF16),
                   jax.ShapeDtypeStruct((1, d), F32),
                   jax.ShapeDtypeStruct((CONV_W - 1, d), F32)],
        scratch_shapes=[pltpu.VMEM((tm + SUBLANES, d), F32),
                        pltpu.VMEM((tm, d), F32),
                        pltpu.VMEM((tm, d), F32),
                        pltpu.VMEM((1, d), F32)],
        compiler_params=_params(),
        name="rg_prompt",
    )(x, g, w_in, cw, cb, wa, ba, wx, bx, lam, mk, mv)


def _rg_sample_kernel(x_ref, g_ref, win_ref, cw_ref, cb_ref, wa_ref, ba_ref, wx_ref, bx_ref,
                      lam_ref, prev_ref, h0_ref,
                      yrg_ref, mq_ref, hnew_ref, conv_ref, a_ref, u_ref):
    d = a_ref.shape[1]
    z = _bdot(_rms(x_ref[...], g_ref[...]), win_ref[...])
    gate = z[:, :d]
    xb = z[:, d:2 * d]
    mq_ref[...] = z[:, 2 * d:]

    cw = cw_ref[...]
    xc = prev_ref[0] * cw[0:1]
    xc = xc + prev_ref[1] * cw[1:2]
    xc = xc + prev_ref[2] * cw[2:3]
    xc = xc + xb * cw[3:4]
    xc = xc + cb_ref[...]
    conv_ref[0] = prev_ref[1]
    conv_ref[1] = prev_ref[2]
    conv_ref[2] = xb

    sp = _softplus(-lam_ref[...])
    _rg_gates(xc, wa_ref, ba_ref[...], wx_ref, bx_ref[...], sp, a_ref, u_ref)
    h = u_ref[...] + a_ref[...] * h0_ref[...]
    hnew_ref[...] = h
    yrg_ref[...] = (h * _gelu_tanh(gate)).astype(BF16)


def _rg_sample(x, g, w_in, cw, cb, wa, ba, wx, bx, lam, prev, h0):
    b, d = x.shape
    args = (x, g, w_in, cw, cb, wa, ba, wx, bx, lam, prev, h0)
    return pl.pallas_call(
        _rg_sample_kernel,
        grid=(1,),
        in_specs=[_const_spec(a.shape) for a in args],
        out_specs=[_const_spec((b, d)), _const_spec((b, MEM_W)), _const_spec((b, d)),
                   _const_spec((CONV_W - 1, b, d))],
        out_shape=[jax.ShapeDtypeStruct((b, d), BF16),
                   jax.ShapeDtypeStruct((b, MEM_W), F32),
                   jax.ShapeDtypeStruct((b, d), F32),
                   jax.ShapeDtypeStruct((CONV_W - 1, b, d), F32)],
        scratch_shapes=[pltpu.VMEM((b, d), F32), pltpu.VMEM((b, d), F32)],
        compiler_params=_params(),
        name="rg_sample",
    )(*args)


MEM_Q_ROWS = 2 * SUBLANES


def _mem_sample_kernel(q_ref, k_ref, v_ref, o_ref):
    g, rows, _ = k_ref.shape
    q = (q_ref[...] * MEM_SCALE).astype(BF16)
    s = lax.dot_general(q, k_ref[...].astype(BF16), BATCH_NT, preferred_element_type=F32)
    q_head = lax.broadcasted_iota(jnp.int32, (g, MEM_Q_ROWS, rows), 1)
    k_head = lax.broadcasted_iota(jnp.int32, (g, MEM_Q_ROWS, rows), 2) & (MEM_HEADS - 1)
    s = jnp.where(q_head == k_head, s, NEG_BIG)
    p = jnp.exp(s - jnp.max(s, axis=-1, keepdims=True))
    l = jnp.sum(p, axis=-1, keepdims=True)
    o_ref[...] = lax.dot_general(p.astype(BF16), v_ref[...].astype(BF16), BATCH_NN,
                                 preferred_element_type=F32) / l


def _mem_sample(mq, cache_k, cache_v, layer, *, group):
    depth, b, n_mem, heads, hd = cache_k.shape
    rows = n_mem * heads
    k = cache_k.reshape(depth * b, rows, hd)
    v = cache_v.reshape(depth * b, rows, hd)
    q = jnp.pad(mq.reshape(b, heads, hd), ((0, 0), (0, MEM_Q_ROWS - heads), (0, 0)))
    base = layer * (b // group)
    out = pl.pallas_call(
        _mem_sample_kernel,
        grid=(b // group,),
        in_specs=[pl.BlockSpec((group, MEM_Q_ROWS, hd), lambda i: (i, 0, 0)),
                  pl.BlockSpec((group, rows, hd), lambda i: (base + i, 0, 0)),
                  pl.BlockSpec((group, rows, hd), lambda i: (base + i, 0, 0))],
        out_specs=pl.BlockSpec((group, MEM_Q_ROWS, hd), lambda i: (i, 0, 0)),
        out_shape=jax.ShapeDtypeStruct((b, MEM_Q_ROWS, hd), F32),
        compiler_params=_params(),
        name="mem_sample",
    )(q, k, v)
    return out[:, :heads, :].reshape(b, heads * hd)


def _mla_pre_kernel(x_ref, g_ref, win_ref, qn_ref, kvn_ref, wqn_ref, wqp_ref, wqs_ref, wuk_ref,
                    inv_ref, sign_ref, *rest, tm, pos_base, pos_stride, shared_mem):
    if shared_mem:
        mk_ref, mv_ref, ckv_ref, kpe_ref, kcat_ref, qcat_ref, mem_ref = rest
    else:
        ckv_ref, kpe_ref, kcat_ref, qcat_ref, mem_ref = rest
    o_kv = Q_LORA
    o_mq = Q_LORA + KV_LORA
    o_pe = o_mq + MEM_W
    o_ps = o_pe + ROPE_PAD

    z = _bdot(_rms(x_ref[...], g_ref[...]), win_ref[...])
    cq = _rms(z[:, :o_kv], qn_ref[...])
    ckv = _rms(z[:, o_kv:o_mq], kvn_ref[...])
    mq = z[:, o_mq:o_pe]

    t = pl.program_id(0) * tm + lax.broadcasted_iota(jnp.int32, (tm, ROPE_PAD), 0)
    pos = (pos_base + pos_stride * t).astype(F32)
    ang = pos * inv_ref[...]
    cos = jnp.cos(ang)
    sin = jnp.sin(ang) * sign_ref[...]

    kpe = z[:, o_pe:o_ps] * cos + z[:, o_ps:] * sin
    ckv_ref[...] = ckv
    kpe_ref[...] = kpe[:, :QK_ROPE]
    kcat_ref[...] = jnp.concatenate([ckv, kpe], axis=-1).astype(BF16)

    cqb = cq.astype(BF16)
    q_nope = jnp.dot(cqb, wqn_ref[...], preferred_element_type=F32)
    q_rope = jnp.dot(cqb, wqp_ref[...], preferred_element_type=F32)
    q_swap = jnp.dot(cqb, wqs_ref[...], preferred_element_type=F32)
    for h in range(MLA_HEADS):
        ns = slice(h * QK_NOPE, (h + 1) * QK_NOPE)
        ps = slice(h * ROPE_PAD, (h + 1) * ROPE_PAD)
        q_lat = _bdot(q_nope[:, ns], wuk_ref[h])
        q_pe = q_rope[:, ps] * cos + q_swap[:, ps] * sin
        qcat_ref[h] = (jnp.concatenate([q_lat, q_pe], axis=-1) * MLA_SCALE).astype(BF16)

    if shared_mem:
        mem_ref[...] = _shared_mem_attend(mq, mk_ref[...], mv_ref[...]).astype(mem_ref.dtype)
    else:
        mem_ref[...] = mq


def _mla_pre(x, g, w_in, qn, kvn, wqn, wqp, wqs, wuk, inv, sign, mem_kv, *, tm, pos_base, pos_stride):
    s, d = x.shape
    shared_mem = mem_kv is not None
    row = lambda i: (i, 0)
    consts = [g, w_in, qn, kvn, wqn, wqp, wqs, wuk, inv, sign] + (list(mem_kv) if shared_mem else [])
    return pl.pallas_call(
        functools.partial(_mla_pre_kernel, tm=tm, pos_base=pos_base, pos_stride=pos_stride,
                          shared_mem=shared_mem),
        grid=(s // tm,),
        in_specs=[pl.BlockSpec((tm, d), row)] + [_const_spec(a.shape) for a in consts],
        out_specs=[pl.BlockSpec((tm, KV_LORA), row),
                   pl.BlockSpec((tm, QK_ROPE), row),
                   pl.BlockSpec((tm, KCAT), row),
                   pl.BlockSpec((MLA_HEADS, tm, KCAT), lambda i: (0, i, 0)),
                   pl.BlockSpec((tm, MEM_W), row)],
        out_shape=[jax.ShapeDtypeStruct((s, KV_LORA), F32),
                   jax.ShapeDtypeStruct((s, QK_ROPE), F32),
                   jax.ShapeDtypeStruct((s, KCAT), BF16),
                   jax.ShapeDtypeStruct((MLA_HEADS, s, KCAT), BF16),
                   jax.ShapeDtypeStruct((s, MEM_W), BF16 if shared_mem else F32)],
        compiler_params=_params(),
        name="mla_pre",
    )(x, *consts)


def _mla_prompt_attn_kernel(q_ref, k_ref, wuv_ref, y_ref, m_ref, l_ref, acc_ref, s_ref, *, tq, tk):
    i = pl.program_id(0)
    rows = MLA_HEADS * tq
    m_ref[...] = jnp.full_like(m_ref, NEG_BIG)
    l_ref[...] = jnp.zeros_like(l_ref)
    acc_ref[...] = jnp.zeros_like(acc_ref)

    def scores(k_start, slot):
        q = q_ref[...].reshape(rows, KCAT)
        kt = k_ref[pl.ds(k_start, tk), :]
        s_ref[slot] = lax.dot_general(q, kt, NT_DIMS, preferred_element_type=F32)

    def update(k_start, slot, masked):
        s = s_ref[slot]
        if masked:
            q_pos = i * tq + (lax.broadcasted_iota(jnp.int32, (rows, tk), 0) & (tq - 1))
            k_pos = k_start + lax.broadcasted_iota(jnp.int32, (rows, tk), 1)
            s = jnp.where(k_pos <= q_pos, s, NEG_BIG)
        m_prev = m_ref[...]
        m_new = jnp.maximum(m_prev, jnp.max(s, axis=-1, keepdims=True))
        alpha = jnp.exp(m_prev - m_new)
        p = jnp.exp(s - jnp.tile(m_new, (1, tk // LANES)))
        p_sum = p[:, :LANES]
        for c in range(1, tk // LANES):
            p_sum = p_sum + p[:, c * LANES:(c + 1) * LANES]
        l_ref[...] = alpha * l_ref[...] + p_sum
        pv = _bdot(p, k_ref[pl.ds(k_start, tk), :KV_LORA])
        acc_ref[...] = jnp.tile(alpha, (1, KV_LORA // LANES)) * acc_ref[...] + pv
        m_ref[...] = m_new

    n_full = (i * tq) // tk
    scores(0, 0)

    def pair(jj, c):
        k0 = pl.multiple_of(2 * jj * tk, tk)
        scores(k0 + tk, 1)
        update(k0, 0, False)
        scores(k0 + 2 * tk, 0)
        update(k0 + tk, 1, False)
        return c

    lax.fori_loop(0, n_full // 2, pair, 0)
    k_last = pl.multiple_of(n_full * tk, tk)

    @pl.when(n_full % 2 == 0)
    def _():
        update(k_last, 0, True)

    @pl.when(n_full % 2 == 1)
    def _():
        scores(k_last, 1)
        update(k_last - tk, 0, False)
        update(k_last, 1, True)

    o = acc_ref[...] / jnp.sum(l_ref[...], axis=-1, keepdims=True)
    for h in range(MLA_HEADS):
        y_ref[:, h * V_HEAD:(h + 1) * V_HEAD] = _bdot(o[h * tq:(h + 1) * tq], wuv_ref[h]).astype(y_ref.dtype)


def _mla_prompt_attn(qcat, kcat, wuv, *, tq, tk):
    heads, s, _ = qcat.shape
    rows = heads * tq
    return pl.pallas_call(
        functools.partial(_mla_prompt_attn_kernel, tq=tq, tk=tk),
        grid=(s // tq,),
        in_specs=[pl.BlockSpec((heads, tq, KCAT), lambda i: (0, i, 0)),
                  _const_spec(kcat.shape), _const_spec(wuv.shape)],
        out_specs=pl.BlockSpec((tq, heads * V_HEAD), lambda i: (i, 0)),
        out_shape=jax.ShapeDtypeStruct((s, heads * V_HEAD), BF16),
        scratch_shapes=[pltpu.VMEM((rows, LANES), F32), pltpu.VMEM((rows, LANES), F32),
                        pltpu.VMEM((rows, KV_LORA), F32), pltpu.VMEM((2, rows, tk), F32)],
        compiler_params=_params(),
        name="mla_prompt_attn",
    )(qcat, kcat, wuv)


def _mla_sample_attn_kernel(pt_ref, q_ref, knew_ref, ckv_hbm, kpe_hbm, o_ref,
                            ckv_buf, kpe_buf, sems, m_ref, l_ref, acc_ref, *, chunk_pages):
    b = pl.program_id(0)
    c = pl.program_id(1)
    n_b = pl.num_programs(0)
    n_c = pl.num_programs(1)
    step = b * n_c + c
    slot = step % 2

    def copies(bb, cc, sl):
        out = []
        for p in range(chunk_pages):
            page = pt_ref[bb, cc * chunk_pages + p]
            rows = pl.ds(p * PAGE_SIZE, PAGE_SIZE)
            out.append(pltpu.make_async_copy(ckv_hbm.at[page], ckv_buf.at[sl, rows], sems.at[0, sl]))
            out.append(pltpu.make_async_copy(kpe_hbm.at[page], kpe_buf.at[sl, p], sems.at[1, sl]))
        return out

    @pl.when(step == 0)
    def _():
        for cp in copies(b, c, slot):
            cp.start()

    @pl.when(step + 1 < n_b * n_c)
    def _():
        nxt = step + 1
        for cp in copies(nxt // n_c, nxt % n_c, 1 - slot):
            cp.start()

    @pl.when(c == 0)
    def _():
        m_ref[...] = jnp.full_like(m_ref, NEG_BIG)
        l_ref[...] = jnp.zeros_like(l_ref)
        acc_ref[...] = jnp.zeros_like(acc_ref)

    for cp in copies(b, c, slot):
        cp.wait()

    q = q_ref[0]
    q_pe = q[:, KV_LORA:KV_LORA + QK_ROPE]
    kc = ckv_buf[slot].astype(BF16)
    s = lax.dot_general(q[:, :KV_LORA], kc, NT_DIMS, preferred_element_type=F32)
    s = s + jnp.concatenate([_bdot(q_pe, kpe_buf[slot, p].astype(BF16)) for p in range(chunk_pages)], axis=-1)
    m_prev = m_ref[...]
    m_new = jnp.maximum(m_prev, jnp.max(s, axis=-1, keepdims=True))
    alpha = jnp.exp(m_prev - m_new)
    p = jnp.exp(s - m_new)
    l_ref[...] = alpha * l_ref[...] + jnp.sum(p, axis=-1, keepdims=True)
    acc_ref[...] = alpha * acc_ref[...] + _bdot(p, kc)
    m_ref[...] = m_new

    @pl.when(c == n_c - 1)
    def _():
        kn = knew_ref[0].astype(F32)
        s_new = jnp.sum(q.astype(F32) * kn, axis=-1, keepdims=True)
        m_prev = m_ref[...]
        m_new = jnp.maximum(m_prev, s_new)
        alpha = jnp.exp(m_prev - m_new)
        p_new = jnp.exp(s_new - m_new)
        l = alpha * l_ref[...] + p_new
        acc = alpha * acc_ref[...] + p_new.astype(BF16).astype(F32) * kn[:, :KV_LORA]
        o_ref[0] = acc / l


def _mla_sample_attn(page_table, qcat, knew, cache_ckv, cache_kpe, *, chunk_pages):
    b, heads, _ = qcat.shape
    n_pages = page_table.shape[1]
    n_c = n_pages // chunk_pages
    chunk = chunk_pages * PAGE_SIZE
    grid_spec = pltpu.PrefetchScalarGridSpec(
        num_scalar_prefetch=1,
        grid=(b, n_c),
        in_specs=[pl.BlockSpec((1, heads, KCAT), lambda i, c, pt: (i, 0, 0)),
                  pl.BlockSpec((1, 1, KCAT), lambda i, c, pt: (i, 0, 0)),
                  pl.BlockSpec(memory_space=pl.ANY),
                  pl.BlockSpec(memory_space=pl.ANY)],
        out_specs=pl.BlockSpec((1, heads, KV_LORA), lambda i, c, pt: (i, 0, 0)),
        scratch_shapes=[pltpu.VMEM((2, chunk, KV_LORA), F32),
                        pltpu.VMEM((2, chunk_pages, QK_ROPE, PAGE_SIZE), F32),
                        pltpu.SemaphoreType.DMA((2, 2)),
                        pltpu.VMEM((heads, 1), F32), pltpu.VMEM((heads, 1), F32),
                        pltpu.VMEM((heads, KV_LORA), F32)],
    )
    return pl.pallas_call(
        functools.partial(_mla_sample_attn_kernel, chunk_pages=chunk_pages),
        grid_spec=grid_spec,
        out_shape=jax.ShapeDtypeStruct((b, heads, KV_LORA), F32),
        compiler_params=_params(2),
        name="mla_sample_attn",
    )(page_table, qcat, knew.reshape(b, 1, KCAT), cache_ckv, cache_kpe)


def _uv_kernel(o_ref, wuv_ref, y_ref):
    for h in range(MLA_HEADS):
        y_ref[:, h * V_HEAD:(h + 1) * V_HEAD] = _bdot(o_ref[h], wuv_ref[h]).astype(y_ref.dtype)


def _uv_project(o_hm, wuv):
    heads, b, _ = o_hm.shape
    return pl.pallas_call(
        _uv_kernel,
        grid=(1,),
        in_specs=[_const_spec(o_hm.shape), _const_spec(wuv.shape)],
        out_specs=_const_spec((b, heads * V_HEAD)),
        out_shape=jax.ShapeDtypeStruct((b, heads * V_HEAD), BF16),
        compiler_params=_params(),
        name="uv_project",
    )(o_hm, wuv)


def _out_ffn_kernel(x_ref, ya_ref, ym_ref, wo_ref, gpost_ref, gpre_ref, wgu_ref, wd_ref, gfpost_ref, o_ref):
    da = ya_ref.shape[1]
    dff = wd_ref.shape[0]
    mix = _bdot(ya_ref[...], wo_ref[:da, :]) + _bdot(ym_ref[...], wo_ref[da:, :])
    x1 = x_ref[...] + _rms(mix, gpost_ref[...])
    hb = _rms(x1, gpre_ref[...]).astype(BF16)
    ffn = None
    for c0 in range(0, dff, FFN_CHUNK):
        c1 = min(c0 + FFN_CHUNK, dff)
        gate = jnp.dot(hb, wgu_ref[:, c0:c1], preferred_element_type=F32)
        up = jnp.dot(hb, wgu_ref[:, dff + c0:dff + c1], preferred_element_type=F32)
        part = _bdot((gate * jax.nn.sigmoid(gate)) * up, wd_ref[c0:c1, :])
        ffn = part if ffn is None else ffn + part
    o_ref[...] = x1 + _rms(ffn, gfpost_ref[...])


def _layer_spec(stacked, layer):
    tail = stacked.shape[1:]
    return pl.BlockSpec((None,) + tail, lambda *_: (layer,) + (0,) * len(tail), pipeline_mode=pl.Buffered(1))


def _out_ffn(x, ya, ym, wo, gpost, gpre, wgu, wd, gfpost, layer, *, tm):
    s, d = x.shape
    row = lambda i: (i, 0)
    consts = (wo, gpost, gpre, wgu, wd, gfpost)
    const_specs = [_const_spec(wo.shape), _const_spec(gpost.shape), _const_spec(gpre.shape),
                   _layer_spec(wgu, layer), _layer_spec(wd, layer), _const_spec(gfpost.shape)]
    return pl.pallas_call(
        _out_ffn_kernel,
        grid=(s // tm,),
        in_specs=[pl.BlockSpec((tm, d), row),
                  pl.BlockSpec((tm, ya.shape[1]), row),
                  pl.BlockSpec((tm, ym.shape[1]), row)] + const_specs,
        out_specs=pl.BlockSpec((tm, d), row),
        out_shape=jax.ShapeDtypeStruct((s, d), F32),
        compiler_params=_params(),
        name="out_ffn",
    )(x, ya, ym, *consts)


def _pad_cols(w, width):
    return jnp.pad(w, ((0, 0), (0, width - w.shape[1])))


def _swap_halves(w):
    half = w.shape[-1] // 2
    return jnp.concatenate([w[..., half:], w[..., :half]], axis=-1)


def _mla_weights(w_in, w_q_up, w_uk, w_uv):
    o2, o3 = Q_LORA + KV_LORA, Q_LORA + KV_LORA + QK_ROPE
    w_pe = w_in[:, o2:o3]
    w_in_ext = jnp.concatenate(
        [w_in[:, :o2], w_in[:, o3:], _pad_cols(w_pe, ROPE_PAD), _pad_cols(_swap_halves(w_pe), ROPE_PAD)],
        axis=1).astype(BF16)
    wq_nope = w_q_up[:, :, :QK_NOPE].reshape(Q_LORA, MLA_HEADS * QK_NOPE).astype(BF16)
    wq_pe = w_q_up[:, :, QK_NOPE:]
    pad = ((0, 0), (0, 0), (0, ROPE_PAD - QK_ROPE))
    wq_rope = jnp.pad(wq_pe, pad).reshape(Q_LORA, MLA_HEADS * ROPE_PAD).astype(BF16)
    wq_swap = jnp.pad(_swap_halves(wq_pe), pad).reshape(Q_LORA, MLA_HEADS * ROPE_PAD).astype(BF16)
    wuk_t = jnp.transpose(w_uk, (1, 2, 0)).astype(BF16)
    wuv_t = jnp.transpose(w_uv, (1, 0, 2)).astype(BF16)
    return w_in_ext, wq_nope, wq_rope, wq_swap, wuk_t, wuv_t


def kernel(x_prompt, x_sample, mem_prompt, state_rg_h, state_rg_conv, cache_ckv, cache_kpe, cache_mem_k, cache_mem_v, page_table, norm_mix_pre, norm_mix_post, norm_ffn_pre, norm_ffn_post, norm_mem, w_mem_kv, w_ffn_gate_up, w_ffn_down, rg_w_in, rg_conv_w, rg_conv_b, rg_gate_a_w, rg_gate_a_b, rg_gate_x_w, rg_gate_x_b, rg_lambda, rg_w_out, mla_w_in, mla_q_norm, mla_kv_norm, mla_w_q_up, mla_w_uk, mla_w_uv, mla_w_out):
    bp, sp, d = x_prompt.shape
    bs, ss, _ = x_sample.shape
    depth = norm_mix_pre.shape[0]
    n_mem = mem_prompt.shape[1]
    assert bp == 1 and ss == 1 and depth == 2
    past = page_table.shape[1] * PAGE_SIZE
    tm = 512

    row = lambda a, l: a[l].reshape(1, -1)
    xp = x_prompt.reshape(sp, d)
    xs = x_sample.reshape(bs, d)

    mk_p, mv_p = _memkv(mem_prompt.reshape(n_mem, d), norm_mem.reshape(depth, 1, d), w_mem_kv.astype(BF16))

    half = jnp.arange(0, QK_ROPE, 2, dtype=F32) / QK_ROPE
    inv = ROPE_THETA ** (-half)
    zeros = jnp.zeros((ROPE_PAD - QK_ROPE,), F32)
    inv_pad = jnp.concatenate([inv, inv, zeros]).reshape(1, ROPE_PAD)
    sign_pad = jnp.concatenate([-jnp.ones_like(inv), jnp.ones_like(inv), zeros]).reshape(1, ROPE_PAD)

    wgu_all = w_ffn_gate_up.astype(BF16)
    wd_all = w_ffn_down.astype(BF16)

    def ffn_args(l):
        return (row(norm_mix_post, l), row(norm_ffn_pre, l), wgu_all, wd_all, row(norm_ffn_post, l), l)

    rg = (row(norm_mix_pre, 0), rg_w_in[0].astype(BF16), rg_conv_w[0], row(rg_conv_b, 0),
          rg_gate_a_w[0].astype(BF16), row(rg_gate_a_b, 0), rg_gate_x_w[0].astype(BF16), row(rg_gate_x_b, 0),
          row(rg_lambda, 0))
    wo0 = rg_w_out[0].astype(BF16)
    yrg_p, ymem_p, p_h, p_conv = _rg_prompt(xp, *rg, mk_p[0], mv_p[0], tm=tm)
    xp = _out_ffn(xp, yrg_p, ymem_p, wo0, *ffn_args(0), tm=tm)

    prev_s = jnp.transpose(state_rg_conv[0], (1, 0, 2))
    yrg_s, mq_s, s_h, s_conv = _rg_sample(xs, *rg, prev_s, state_rg_h[0])
    ymem_s = _mem_sample(mq_s, cache_mem_k, cache_mem_v, 0, group=8)
    xs = _out_ffn(xs, yrg_s, ymem_s, wo0, *ffn_args(0), tm=bs)

    w_in_ext, wq_nope, wq_rope, wq_swap, wuk_t, wuv_t = _mla_weights(mla_w_in[0], mla_w_q_up[0], mla_w_uk[0], mla_w_uv[0])
    mla = (row(norm_mix_pre, 1), w_in_ext, row(mla_q_norm, 0), row(mla_kv_norm, 0),
           wq_nope, wq_rope, wq_swap, wuk_t, inv_pad, sign_pad)
    wo1 = mla_w_out[0].astype(BF16)

    p_ckv, p_kpe, kcat_p, qcat_p, ymem_p = _mla_pre(xp, *mla, (mk_p[1], mv_p[1]), tm=tm, pos_base=0, pos_stride=1)
    yatt_p = _mla_prompt_attn(qcat_p, kcat_p, wuv_t, tq=128, tk=1024)
    xp = _out_ffn(xp, yatt_p, ymem_p, wo1, *ffn_args(1), tm=tm)

    s_ckv, s_kpe, kcat_s, qcat_s, mq_s = _mla_pre(xs, *mla, None, tm=bs, pos_base=past, pos_stride=0)
    ymem_s = _mem_sample(mq_s, cache_mem_k, cache_mem_v, 1, group=8)
    kpe_pages = jnp.swapaxes(cache_kpe[0], 1, 2)
    o_lat = _mla_sample_attn(page_table, jnp.transpose(qcat_s, (1, 0, 2)), kcat_s, cache_ckv[0], kpe_pages,
                             chunk_pages=64)
    yatt_s = _uv_project(jnp.transpose(o_lat, (1, 0, 2)).astype(BF16), wuv_t)
    xs = _out_ffn(xs, yatt_s, ymem_s, wo1, *ffn_args(1), tm=bs)

    mem_shape = (depth, bp, n_mem, MEM_HEADS, MEM_HD)
    return (xp.reshape(bp, sp, d), xs.reshape(bs, ss, d),
            p_h.reshape(1, bp, d), p_conv.reshape(1, bp, CONV_W - 1, d),
            p_ckv.reshape(1, bp, sp, KV_LORA), p_kpe.reshape(1, bp, sp, QK_ROPE),
            mk_p.reshape(mem_shape), mv_p.reshape(mem_shape),
            s_h.reshape(1, bs, d), jnp.transpose(s_conv, (1, 0, 2)).reshape(1, bs, CONV_W - 1, d),
            s_ckv.reshape(1, bs, ss, KV_LORA), s_kpe.reshape(1, bs, ss, QK_ROPE))
```

```python
import functools

import jax
import jax.numpy as jnp
from jax import lax
from jax.experimental import pallas as pl
from jax.experimental.pallas import tpu as pltpu

F32 = jnp.float32
BF16 = jnp.bfloat16

EPS = 1e-6
RG_C = 8.0
RG_BLOCKS = 8
RG_BLOCK_W = 128
CONV_W = 4
MLA_HEADS = 8
Q_LORA = 512
KV_LORA = 256
QK_NOPE = 128
QK_ROPE = 64
V_HEAD = 128
ROPE_THETA = 10000.0
MLA_SCALE = (QK_NOPE + QK_ROPE) ** -0.5
MEM_HEADS = 4
MEM_HD = 128
MEM_W = MEM_HEADS * MEM_HD
MEM_SCALE = MEM_HD ** -0.5
PAGE_SIZE = 128

LANES = 128
SUBLANES = 8
ROPE_PAD = LANES
KCAT = KV_LORA + ROPE_PAD
VMEM_LIMIT = 56 * 1024 * 1024
NEG_BIG = -1e30
FFN_CHUNK = 1024

NT_DIMS = (((1,), (1,)), ((), ()))
BATCH_NT = (((2,), (2,)), ((0,), (0,)))
BATCH_NN = (((2,), (1,)), ((0,), (0,)))


def _const_spec(shape):
    nd = len(shape)
    return pl.BlockSpec(shape, lambda *_: (0,) * nd, pipeline_mode=pl.Buffered(1))


def _params(n_axes=1):
    return pltpu.CompilerParams(dimension_semantics=("arbitrary",) * n_axes,
                                vmem_limit_bytes=VMEM_LIMIT)


def _rms(x, g):
    return x * lax.rsqrt(jnp.mean(x * x, axis=-1, keepdims=True) + EPS) * g


def _bdot(a, b):
    return jnp.dot(a.astype(BF16), b, preferred_element_type=F32)


def _gelu_tanh(x):
    return 0.5 * x * (1.0 + jnp.tanh(0.7978845608028654 * (x + 0.044715 * (x * x * x))))


def _softplus(x):
    return jnp.maximum(x, 0.0) + jnp.log1p(jnp.exp(-jnp.abs(x)))


def _shared_mem_attend(mq, mk, mv):
    outs = []
    for h in range(MEM_HEADS):
        hs = slice(h * MEM_HD, (h + 1) * MEM_HD)
        q = (mq[:, hs] * MEM_SCALE).astype(BF16)
        s = lax.dot_general(q, mk[:, hs].astype(BF16), NT_DIMS, preferred_element_type=F32)
        p = jnp.exp(s - jnp.max(s, axis=-1, keepdims=True))
        l = jnp.sum(p, axis=-1, keepdims=True)
        outs.append(_bdot(p, mv[:, hs].astype(BF16)) / l)
    return jnp.concatenate(outs, axis=-1)


def _memkv_kernel(mem_ref, g_ref, w_ref, mk_ref, mv_ref):
    kv = _bdot(_rms(mem_ref[...], g_ref[0]), w_ref[0])
    mk_ref[0] = kv[:, :MEM_W]
    mv_ref[0] = kv[:, MEM_W:]


def _memkv(mem, g, w):
    depth = w.shape[0]
    n_mem, d = mem.shape
    out = jax.ShapeDtypeStruct((depth, n_mem, MEM_W), F32)
    return pl.pallas_call(
        _memkv_kernel,
        grid=(depth,),
        in_specs=[_const_spec((n_mem, d)),
                  pl.BlockSpec((1, 1, d), lambda l: (l, 0, 0)),
                  pl.BlockSpec((1, d, 2 * MEM_W), lambda l: (l, 0, 0))],
        out_specs=[pl.BlockSpec((1, n_mem, MEM_W), lambda l: (l, 0, 0))] * 2,
        out_shape=[out, out],
        compiler_params=_params(),
        name="memkv",
    )(mem, g, w)


def _rg_gates(xc, wa_ref, ba, wx_ref, bx, sp, a_out, u_out):
    xcb = xc.astype(BF16)
    for n in range(RG_BLOCKS):
        bs = slice(n * RG_BLOCK_W, (n + 1) * RG_BLOCK_W)
        r = jax.nn.sigmoid(jnp.dot(xcb[:, bs], wa_ref[n], preferred_element_type=F32) + ba[:, bs])
        i = jax.nn.sigmoid(jnp.dot(xcb[:, bs], wx_ref[n], preferred_element_type=F32) + bx[:, bs])
        log_a = -RG_C * r * sp[:, bs]
        a = jnp.exp(log_a)
        a_out[:, bs] = a
        u_out[:, bs] = jnp.sqrt(-jnp.tanh(log_a) * (1.0 + a * a)) * (i * xc[:, bs])


def _rg_prompt_kernel(x_ref, g_ref, win_ref, cw_ref, cb_ref, wa_ref, ba_ref, wx_ref, bx_ref,
                      lam_ref, mk_ref, mv_ref,
                      yrg_ref, ymem_ref, hlast_ref, conv_ref,
                      xb_ref, a_ref, u_ref, h_ref, *, tm):
    d = a_ref.shape[1]
    hist = SUBLANES

    @pl.when(pl.program_id(0) == 0)
    def _():
        xb_ref[0:hist, :] = jnp.zeros((hist, d), F32)
        h_ref[...] = jnp.zeros_like(h_ref)

    z = _bdot(_rms(x_ref[...], g_ref[...]), win_ref[...])
    gate = z[:, :d]
    xb = z[:, d:2 * d]
    mq = z[:, 2 * d:]

    xb_ref[hist:hist + tm, :] = xb
    cw = cw_ref[...]
    xc = xb_ref[hist - 3:hist - 3 + tm, :] * cw[0:1]
    xc = xc + xb_ref[hist - 2:hist - 2 + tm, :] * cw[1:2]
    xc = xc + xb_ref[hist - 1:hist - 1 + tm, :] * cw[2:3]
    xc = xc + xb * cw[3:4]
    xc = xc + cb_ref[...]
    tail = xb_ref[tm + hist - 3:tm + hist, :]
    conv_ref[...] = tail
    xb_ref[hist - 3:hist, :] = tail

    sp = _softplus(-lam_ref[...])
    _rg_gates(xc, wa_ref, ba_ref[...], wx_ref, bx_ref[...], sp, a_ref, u_ref)

    row = lax.broadcasted_iota(jnp.int32, (SUBLANES, d), 0)

    def group(g, h):
        r0 = pl.multiple_of(g * SUBLANES, SUBLANES)
        a = a_ref[pl.ds(r0, SUBLANES), :]
        b = u_ref[pl.ds(r0, SUBLANES), :]
        for s in (1, 2, 4):
            keep = row >= s
            b = jnp.where(keep, a * pltpu.roll(b, s, axis=0) + b, b)
            a = jnp.where(keep, a * pltpu.roll(a, s, axis=0), a)
        hs = a * h + b
        u_ref[pl.ds(r0, SUBLANES), :] = hs
        return hs[SUBLANES - 1:SUBLANES, :]

    h_end = lax.fori_loop(0, tm // SUBLANES, group, h_ref[...])
    h_ref[...] = h_end
    hlast_ref[...] = h_end

    yrg_ref[...] = (u_ref[...] * _gelu_tanh(gate)).astype(BF16)
    ymem_ref[...] = _shared_mem_attend(mq, mk_ref[...], mv_ref[...]).astype(BF16)


def _rg_prompt(x, g, w_in, cw, cb, wa, ba, wx, bx, lam, mk, mv, *, tm):
    s, d = x.shape
    rg_in = w_in.shape[1]
    row = lambda i: (i, 0)
    return pl.pallas_call(
        functools.partial(_rg_prompt_kernel, tm=tm),
        grid=(s // tm,),
        in_specs=[pl.BlockSpec((tm, d), row),
                  _const_spec((1, d)), _const_spec((d, rg_in)),
                  _const_spec((CONV_W, d)), _const_spec((1, d)),
                  _const_spec(wa.shape), _const_spec((1, d)),
                  _const_spec(wx.shape), _const_spec((1, d)),
                  _const_spec((1, d)),
                  _const_spec(mk.shape), _const_spec(mv.shape)],
        out_specs=[pl.BlockSpec((tm, d), row),
                   pl.BlockSpec((tm, MEM_W), row),
                   pl.BlockSpec((1, d), lambda i: (0, 0)),
                   pl.BlockSpec((CONV_W - 1, d), lambda i: (0, 0))],
        out_shape=[jax.ShapeDtypeStruct((s, d), BF16),
                   jax.ShapeDtypeStruct((s, MEM_W), BF16),
                   jax.ShapeDtypeStruct((1, d), F32),
                   jax.ShapeDtypeStruct((CONV_W - 1, d), F32)],
        scratch_shapes=[pltpu.VMEM((tm + SUBLANES, d), F32),
                        pltpu.VMEM((tm, d), F32),
                        pltpu.VMEM((tm, d), F32),
                        pltpu.VMEM((1, d), F32)],
        compiler_params=_params(),
        name="rg_prompt",
    )(x, g, w_in, cw, cb, wa, ba, wx, bx, lam, mk, mv)


def _rg_sample_kernel(x_ref, g_ref, win_ref, cw_ref, cb_ref, wa_ref, ba_ref, wx_ref, bx_ref,
                      lam_ref, prev_ref, h0_ref,
                      yrg_ref, mq_ref, hnew_ref, conv_ref, a_ref, u_ref):
    d = a_ref.shape[1]
    z = _bdot(_rms(x_ref[...], g_ref[...]), win_ref[...])
    gate = z[:, :d]
    xb = z[:, d:2 * d]
    mq_ref[...] = z[:, 2 * d:]

    cw = cw_ref[...]
    xc = prev_ref[0] * cw[0:1]
    xc = xc + prev_ref[1] * cw[1:2]
    xc = xc + prev_ref[2] * cw[2:3]
    xc = xc + xb * cw[3:4]
    xc = xc + cb_ref[...]
    conv_ref[0] = prev_ref[1]
    conv_ref[1] = prev_ref[2]
    conv_ref[2] = xb

    sp = _softplus(-lam_ref[...])
    _rg_gates(xc, wa_ref, ba_ref[...], wx_ref, bx_ref[...], sp, a_ref, u_ref)
    h = u_ref[...] + a_ref[...] * h0_ref[...]
    hnew_ref[...] = h
    yrg_ref[...] = (h * _gelu_tanh(gate)).astype(BF16)


def _rg_sample(x, g, w_in, cw, cb, wa, ba, wx, bx, lam, prev, h0):
    b, d = x.shape
    args = (x, g, w_in, cw, cb, wa, ba, wx, bx, lam, prev, h0)
    return pl.pallas_call(
        _rg_sample_kernel,
        grid=(1,),
        in_specs=[_const_spec(a.shape) for a in args],
        out_specs=[_const_spec((b, d)), _const_spec((b, MEM_W)), _const_spec((b, d)),
                   _const_spec((CONV_W - 1, b, d))],
        out_shape=[jax.ShapeDtypeStruct((b, d), BF16),
                   jax.ShapeDtypeStruct((b, MEM_W), F32),
                   jax.ShapeDtypeStruct((b, d), F32),
                   jax.ShapeDtypeStruct((CONV_W - 1, b, d), F32)],
        scratch_shapes=[pltpu.VMEM((b, d), F32), pltpu.VMEM((b, d), F32)],
        compiler_params=_params(),
        name="rg_sample",
    )(*args)


MEM_Q_ROWS = 2 * SUBLANES


def _mem_sample_kernel(q_ref, k_ref, v_ref, o_ref):
    g, rows, _ = k_ref.shape
    q = (q_ref[...] * MEM_SCALE).astype(BF16)
    s = lax.dot_general(q, k_ref[...].astype(BF16), BATCH_NT, preferred_element_type=F32)
    q_head = lax.broadcasted_iota(jnp.int32, (g, MEM_Q_ROWS, rows), 1)
    k_head = lax.broadcasted_iota(jnp.int32, (g, MEM_Q_ROWS, rows), 2) & (MEM_HEADS - 1)
    s = jnp.where(q_head == k_head, s, NEG_BIG)
    p = jnp.exp(s - jnp.max(s, axis=-1, keepdims=True))
    l = jnp.sum(p, axis=-1, keepdims=True)
    o_ref[...] = lax.dot_general(p.astype(BF16), v_ref[...].astype(BF16), BATCH_NN,
                                 preferred_element_type=F32) / l


def _mem_sample(mq, cache_k, cache_v, layer, *, group):
    depth, b, n_mem, heads, hd = cache_k.shape
    rows = n_mem * heads
    k = cache_k.reshape(depth * b, rows, hd)
    v = cache_v.reshape(depth * b, rows, hd)
    q = jnp.pad(mq.reshape(b, heads, hd), ((0, 0), (0, MEM_Q_ROWS - heads), (0, 0)))
    base = layer * (b // group)
    out = pl.pallas_call(
        _mem_sample_kernel,
        grid=(b // group,),
        in_specs=[pl.BlockSpec((group, MEM_Q_ROWS, hd), lambda i: (i, 0, 0)),
                  pl.BlockSpec((group, rows, hd), lambda i: (base + i, 0, 0)),
                  pl.BlockSpec((group, rows, hd), lambda i: (base + i, 0, 0))],
        out_specs=pl.BlockSpec((group, MEM_Q_ROWS, hd), lambda i: (i, 0, 0)),
        out_shape=jax.ShapeDtypeStruct((b, MEM_Q_ROWS, hd), F32),
        compiler_params=_params(),
        name="mem_sample",
    )(q, k, v)
    return out[:, :heads, :].reshape(b, heads * hd)


def _mla_pre_kernel(x_ref, g_ref, win_ref, qn_ref, kvn_ref, wqn_ref, wqp_ref, wqs_ref, wuk_ref,
                    inv_ref, sign_ref, *rest, tm, pos_base, pos_stride, prompt):
    if prompt:
        mk_ref, mv_ref, ckv_ref, kpe_ref, kshared_ref, q_ref, mem_ref, kheads_ref = rest
    else:
        ckv_ref, kpe_ref, kshared_ref, q_ref, mem_ref = rest
    o_kv = Q_LORA
    o_mq = Q_LORA + KV_LORA
    o_pe = o_mq + MEM_W
    o_ps = o_pe + ROPE_PAD

    z = _bdot(_rms(x_ref[...], g_ref[...]), win_ref[...])
    cq = _rms(z[:, :o_kv], qn_ref[...])
    ckv = _rms(z[:, o_kv:o_mq], kvn_ref[...])
    mq = z[:, o_mq:o_pe]

    t = pl.program_id(0) * tm + lax.broadcasted_iota(jnp.int32, (tm, ROPE_PAD), 0)
    pos = (pos_base + pos_stride * t).astype(F32)
    ang = pos * inv_ref[...]
    cos = jnp.cos(ang)
    sin = jnp.sin(ang) * sign_ref[...]

    kpe = z[:, o_pe:o_ps] * cos + z[:, o_ps:] * sin
    ckv_ref[...] = ckv
    kpe_ref[...] = kpe[:, :QK_ROPE]
    ckv_b = ckv.astype(BF16)
    kpe_b = kpe.astype(BF16)

    cqb = cq.astype(BF16)
    q_nope = jnp.dot(cqb, wqn_ref[...], preferred_element_type=F32)
    q_rope = jnp.dot(cqb, wqp_ref[...], preferred_element_type=F32)
    q_swap = jnp.dot(cqb, wqs_ref[...], preferred_element_type=F32)
    if prompt:
        kshared_ref[...] = ckv_b
        k_nope = jnp.dot(ckv_b, wuk_ref[...], preferred_element_type=F32)
    else:
        kshared_ref[...] = jnp.concatenate([ckv_b, kpe_b], axis=-1)
    for h in range(MLA_HEADS):
        ns = slice(h * QK_NOPE, (h + 1) * QK_NOPE)
        ps = slice(h * ROPE_PAD, (h + 1) * ROPE_PAD)
        q_pe = q_rope[:, ps] * cos + q_swap[:, ps] * sin
        if prompt:
            q_main = q_nope[:, ns]
            kheads_ref[h] = jnp.concatenate([k_nope[:, ns].astype(BF16), kpe_b], axis=-1)
        else:
            q_main = _bdot(q_nope[:, ns], wuk_ref[h])
        q_ref[h] = (jnp.concatenate([q_main, q_pe], axis=-1) * MLA_SCALE).astype(BF16)

    if prompt:
        mem_ref[...] = _shared_mem_attend(mq, mk_ref[...], mv_ref[...]).astype(mem_ref.dtype)
    else:
        mem_ref[...] = mq


def _mla_pre(x, g, w_in, qn, kvn, wqn, wqp, wqs, wuk, inv, sign, mem_kv, *, tm, pos_base, pos_stride):
    s, d = x.shape
    prompt = mem_kv is not None
    row = lambda i: (i, 0)
    heads_row = lambda i: (0, i, 0)
    consts = [g, w_in, qn, kvn, wqn, wqp, wqs, wuk, inv, sign] + (list(mem_kv) if prompt else [])
    qw = QK_NOPE + ROPE_PAD if prompt else KCAT
    kw = KV_LORA if prompt else KCAT
    out_specs = [pl.BlockSpec((tm, KV_LORA), row),
                 pl.BlockSpec((tm, QK_ROPE), row),
                 pl.BlockSpec((tm, kw), row),
                 pl.BlockSpec((MLA_HEADS, tm, qw), heads_row),
                 pl.BlockSpec((tm, MEM_W), row)]
    out_shape = [jax.ShapeDtypeStruct((s, KV_LORA), F32),
                 jax.ShapeDtypeStruct((s, QK_ROPE), F32),
                 jax.ShapeDtypeStruct((s, kw), BF16),
                 jax.ShapeDtypeStruct((MLA_HEADS, s, qw), BF16),
                 jax.ShapeDtypeStruct((s, MEM_W), BF16 if prompt else F32)]
    if prompt:
        out_specs.append(pl.BlockSpec((MLA_HEADS, tm, qw), heads_row))
        out_shape.append(jax.ShapeDtypeStruct((MLA_HEADS, s, qw), BF16))
    return pl.pallas_call(
        functools.partial(_mla_pre_kernel, tm=tm, pos_base=pos_base, pos_stride=pos_stride, prompt=prompt),
        grid=(s // tm,),
        in_specs=[pl.BlockSpec((tm, d), row)] + [_const_spec(a.shape) for a in consts],
        out_specs=out_specs,
        out_shape=out_shape,
        compiler_params=_params(),
        name="mla_pre",
    )(x, *consts)


def _mla_prompt_attn_kernel(q_ref, k_ref, v_ref, wuv_ref, y_ref, m_ref, l_ref, acc_ref, s_ref, *, t):
    i = pl.program_id(1)
    m_ref[...] = jnp.full_like(m_ref, NEG_BIG)
    l_ref[...] = jnp.zeros_like(l_ref)
    acc_ref[...] = jnp.zeros_like(acc_ref)

    def scores(k_start, slot):
        s_ref[slot] = lax.dot_general(q_ref[...], k_ref[pl.ds(k_start, t), :], NT_DIMS,
                                      preferred_element_type=F32)

    def update(k_start, slot, masked):
        s = s_ref[slot]
        if masked:
            q_pos = lax.broadcasted_iota(jnp.int32, (t, t), 0)
            k_pos = lax.broadcasted_iota(jnp.int32, (t, t), 1)
            s = jnp.where(k_pos <= q_pos, s, NEG_BIG)
        m_prev = m_ref[...]
        m_new = jnp.maximum(m_prev, jnp.max(s, axis=-1, keepdims=True))
        alpha = jnp.exp(m_prev - m_new)
        p = jnp.exp(s - jnp.tile(m_new, (1, t // LANES)))
        p_sum = p[:, :LANES]
        for c in range(1, t // LANES):
            p_sum = p_sum + p[:, c * LANES:(c + 1) * LANES]
        l_ref[...] = alpha * l_ref[...] + p_sum
        pv = _bdot(p, v_ref[pl.ds(k_start, t), :])
        acc_ref[...] = jnp.tile(alpha, (1, KV_LORA // LANES)) * acc_ref[...] + pv
        m_ref[...] = m_new

    scores(0, 0)

    def pair(jj, c):
        k0 = pl.multiple_of(2 * jj * t, t)
        scores(k0 + t, 1)
        update(k0, 0, False)
        scores(k0 + 2 * t, 0)
        update(k0 + t, 1, False)
        return c

    lax.fori_loop(0, i // 2, pair, 0)
    k_diag = pl.multiple_of(i * t, t)

    @pl.when(i % 2 == 0)
    def _():
        update(k_diag, 0, True)

    @pl.when(i % 2 == 1)
    def _():
        scores(k_diag, 1)
        update(k_diag - t, 0, False)
        update(k_diag, 1, True)

    o = acc_ref[...] / jnp.sum(l_ref[...], axis=-1, keepdims=True)
    y_ref[...] = _bdot(o, wuv_ref[...]).astype(y_ref.dtype)


def _mla_prompt_attn(q_heads, k_heads, v, wuv, *, t):
    heads, s, w = q_heads.shape
    return pl.pallas_call(
        functools.partial(_mla_prompt_attn_kernel, t=t),
        grid=(heads, s // t),
        in_specs=[pl.BlockSpec((None, t, w), lambda h, i: (h, i, 0)),
                  pl.BlockSpec((None, s, w), lambda h, i: (h, 0, 0)),
                  _const_spec(v.shape),
                  pl.BlockSpec((None, KV_LORA, V_HEAD), lambda h, i: (h, 0, 0))],
        out_specs=pl.BlockSpec((t, V_HEAD), lambda h, i: (i, h)),
        out_shape=jax.ShapeDtypeStruct((s, heads * V_HEAD), BF16),
        scratch_shapes=[pltpu.VMEM((t, LANES), F32), pltpu.VMEM((t, LANES), F32),
                        pltpu.VMEM((t, KV_LORA), F32), pltpu.VMEM((2, t, t), F32)],
        compiler_params=_params(2),
        name="mla_prompt_attn",
    )(q_heads, k_heads, v, wuv)


def _mla_sample_attn_kernel(pt_ref, q_ref, knew_ref, ckv_hbm, kpe_hbm, o_ref,
                            ckv_buf, kpe_buf, sems, m_ref, l_ref, acc_ref, *, chunk_pages):
    b = pl.program_id(0)
    c = pl.program_id(1)
    n_b = pl.num_programs(0)
    n_c = pl.num_programs(1)
    step = b * n_c + c
    slot = step % 2

    def copies(bb, cc, sl):
        out = []
        for p in range(chunk_pages):
            page = pt_ref[bb, cc * chunk_pages + p]
            rows = pl.ds(p * PAGE_SIZE, PAGE_SIZE)
            out.append(pltpu.make_async_copy(ckv_hbm.at[page], ckv_buf.at[sl, rows], sems.at[0, sl]))
            out.append(pltpu.make_async_copy(kpe_hbm.at[page], kpe_buf.at[sl, p], sems.at[1, sl]))
        return out

    @pl.when(step == 0)
    def _():
        for cp in copies(b, c, slot):
            cp.start()

    @pl.when(step + 1 < n_b * n_c)
    def _():
        nxt = step + 1
        for cp in copies(nxt // n_c, nxt % n_c, 1 - slot):
            cp.start()

    @pl.when(c == 0)
    def _():
        m_ref[...] = jnp.full_like(m_ref, NEG_BIG)
        l_ref[...] = jnp.zeros_like(l_ref)
        acc_ref[...] = jnp.zeros_like(acc_ref)

    for cp in copies(b, c, slot):
        cp.wait()

    q = q_ref[0]
    q_pe = q[:, KV_LORA:KV_LORA + QK_ROPE]
    kc = ckv_buf[slot].astype(BF16)
    s = lax.dot_general(q[:, :KV_LORA], kc, NT_DIMS, preferred_element_type=F32)
    s = s + jnp.concatenate([_bdot(q_pe, kpe_buf[slot, p].astype(BF16)) for p in range(chunk_pages)], axis=-1)
    m_prev = m_ref[...]
    m_new = jnp.maximum(m_prev, jnp.max(s, axis=-1, keepdims=True))
    alpha = jnp.exp(m_prev - m_new)
    p = jnp.exp(s - m_new)
    l_ref[...] = alpha * l_ref[...] + jnp.sum(p, axis=-1, keepdims=True)
    acc_ref[...] = alpha * acc_ref[...] + _bdot(p, kc)
    m_ref[...] = m_new

    @pl.when(c == n_c - 1)
    def _():
        kn = knew_ref[0].astype(F32)
        s_new = jnp.sum(q.astype(F32) * kn, axis=-1, keepdims=True)
        m_prev = m_ref[...]
        m_new = jnp.maximum(m_prev, s_new)
        alpha = jnp.exp(m_prev - m_new)
        p_new = jnp.exp(s_new - m_new)
        l = alpha * l_ref[...] + p_new
        acc = alpha * acc_ref[...] + p_new.astype(BF16).astype(F32) * kn[:, :KV_LORA]
        o_ref[0] = acc / l


def _mla_sample_attn(page_table, qcat, knew, cache_ckv, cache_kpe, *, chunk_pages):
    b, heads, _ = qcat.shape
    n_pages = page_table.shape[1]
    n_c = n_pages // chunk_pages
    chunk = chunk_pages * PAGE_SIZE
    grid_spec = pltpu.PrefetchScalarGridSpec(
        num_scalar_prefetch=1,
        grid=(b, n_c),
        in_specs=[pl.BlockSpec((1, heads, KCAT), lambda i, c, pt: (i, 0, 0)),
                  pl.BlockSpec((1, 1, KCAT), lambda i, c, pt: (i, 0, 0)),
                  pl.BlockSpec(memory_space=pl.ANY),
                  pl.BlockSpec(memory_space=pl.ANY)],
        out_specs=pl.BlockSpec((1, heads, KV_LORA), lambda i, c, pt: (i, 0, 0)),
        scratch_shapes=[pltpu.VMEM((2, chunk, KV_LORA), F32),
                        pltpu.VMEM((2, chunk_pages, QK_ROPE, PAGE_SIZE), F32),
                        pltpu.SemaphoreType.DMA((2, 2)),
                        pltpu.VMEM((heads, 1), F32), pltpu.VMEM((heads, 1), F32),
                        pltpu.VMEM((heads, KV_LORA), F32)],
    )
    return pl.pallas_call(
        functools.partial(_mla_sample_attn_kernel, chunk_pages=chunk_pages),
        grid_spec=grid_spec,
        out_shape=jax.ShapeDtypeStruct((b, heads, KV_LORA), F32),
        compiler_params=_params(2),
        name="mla_sample_attn",
    )(page_table, qcat, knew.reshape(b, 1, KCAT), cache_ckv, cache_kpe)


def _uv_kernel(o_ref, wuv_ref, y_ref):
    for h in range(MLA_HEADS):
        y_ref[:, h * V_HEAD:(h + 1) * V_HEAD] = _bdot(o_ref[h], wuv_ref[h]).astype(y_ref.dtype)


def _uv_project(o_hm, wuv):
    heads, b, _ = o_hm.shape
    return pl.pallas_call(
        _uv_kernel,
        grid=(1,),
        in_specs=[_const_spec(o_hm.shape), _const_spec(wuv.shape)],
        out_specs=_const_spec((b, heads * V_HEAD)),
        out_shape=jax.ShapeDtypeStruct((b, heads * V_HEAD), BF16),
        compiler_params=_params(),
        name="uv_project",
    )(o_hm, wuv)


def _out_ffn_kernel(x_ref, ya_ref, ym_ref, wo_ref, gpost_ref, gpre_ref, wgu_ref, wd_ref, gfpost_ref, o_ref):
    da = ya_ref.shape[1]
    dff = wd_ref.shape[0]
    mix = _bdot(ya_ref[...], wo_ref[:da, :]) + _bdot(ym_ref[...], wo_ref[da:, :])
    x1 = x_ref[...] + _rms(mix, gpost_ref[...])
    hb = _rms(x1, gpre_ref[...]).astype(BF16)
    ffn = None
    for c0 in range(0, dff, FFN_CHUNK):
        c1 = min(c0 + FFN_CHUNK, dff)
        gate = jnp.dot(hb, wgu_ref[:, c0:c1], preferred_element_type=F32)
        up = jnp.dot(hb, wgu_ref[:, dff + c0:dff + c1], preferred_element_type=F32)
        part = _bdot((gate * jax.nn.sigmoid(gate)) * up, wd_ref[c0:c1, :])
        ffn = part if ffn is None else ffn + part
    o_ref[...] = x1 + _rms(ffn, gfpost_ref[...])


def _layer_spec(stacked, layer):
    tail = stacked.shape[1:]
    return pl.BlockSpec((None,) + tail, lambda *_: (layer,) + (0,) * len(tail), pipeline_mode=pl.Buffered(1))


def _out_ffn(x, ya, ym, wo, gpost, gpre, wgu, wd, gfpost, layer, *, tm):
    s, d = x.shape
    row = lambda i: (i, 0)
    consts = (wo, gpost, gpre, wgu, wd, gfpost)
    const_specs = [_const_spec(wo.shape), _const_spec(gpost.shape), _const_spec(gpre.shape),
                   _layer_spec(wgu, layer), _layer_spec(wd, layer), _const_spec(gfpost.shape)]
    return pl.pallas_call(
        _out_ffn_kernel,
        grid=(s // tm,),
        in_specs=[pl.BlockSpec((tm, d), row),
                  pl.BlockSpec((tm, ya.shape[1]), row),
                  pl.BlockSpec((tm, ym.shape[1]), row)] + const_specs,
        out_specs=pl.BlockSpec((tm, d), row),
        out_shape=jax.ShapeDtypeStruct((s, d), F32),
        compiler_params=_params(),
        name="out_ffn",
    )(x, ya, ym, *consts)


def _pad_cols(w, width):
    return jnp.pad(w, ((0, 0), (0, width - w.shape[1])))


def _swap_halves(w):
    half = w.shape[-1] // 2
    return jnp.concatenate([w[..., half:], w[..., :half]], axis=-1)


def _mla_weights(w_in, w_q_up, w_uk, w_uv):
    o2, o3 = Q_LORA + KV_LORA, Q_LORA + KV_LORA + QK_ROPE
    w_pe = w_in[:, o2:o3]
    w_in_ext = jnp.concatenate(
        [w_in[:, :o2], w_in[:, o3:], _pad_cols(w_pe, ROPE_PAD), _pad_cols(_swap_halves(w_pe), ROPE_PAD)],
        axis=1).astype(BF16)
    wq_nope = w_q_up[:, :, :QK_NOPE].reshape(Q_LORA, MLA_HEADS * QK_NOPE).astype(BF16)
    wq_pe = w_q_up[:, :, QK_NOPE:]
    pad = ((0, 0), (0, 0), (0, ROPE_PAD - QK_ROPE))
    wq_rope = jnp.pad(wq_pe, pad).reshape(Q_LORA, MLA_HEADS * ROPE_PAD).astype(BF16)
    wq_swap = jnp.pad(_swap_halves(wq_pe), pad).reshape(Q_LORA, MLA_HEADS * ROPE_PAD).astype(BF16)
    wuk_t = jnp.transpose(w_uk, (1, 2, 0)).astype(BF16)
    wuk_flat = w_uk.reshape(KV_LORA, MLA_HEADS * QK_NOPE).astype(BF16)
    wuv_t = jnp.transpose(w_uv, (1, 0, 2)).astype(BF16)
    return w_in_ext, wq_nope, wq_rope, wq_swap, wuk_t, wuk_flat, wuv_t


def kernel(x_prompt, x_sample, mem_prompt, state_rg_h, state_rg_conv, cache_ckv, cache_kpe, cache_mem_k, cache_mem_v, page_table, norm_mix_pre, norm_mix_post, norm_ffn_pre, norm_ffn_post, norm_mem, w_mem_kv, w_ffn_gate_up, w_ffn_down, rg_w_in, rg_conv_w, rg_conv_b, rg_gate_a_w, rg_gate_a_b, rg_gate_x_w, rg_gate_x_b, rg_lambda, rg_w_out, mla_w_in, mla_q_norm, mla_kv_norm, mla_w_q_up, mla_w_uk, mla_w_uv, mla_w_out):
    bp, sp, d = x_prompt.shape
    bs, ss, _ = x_sample.shape
    depth = norm_mix_pre.shape[0]
    n_mem = mem_prompt.shape[1]
    assert bp == 1 and ss == 1 and depth == 2
    past = page_table.shape[1] * PAGE_SIZE
    tm = 512

    row = lambda a, l: a[l].reshape(1, -1)
    xp = x_prompt.reshape(sp, d)
    xs = x_sample.reshape(bs, d)

    mk_p, mv_p = _memkv(mem_prompt.reshape(n_mem, d), norm_mem.reshape(depth, 1, d), w_mem_kv.astype(BF16))

    half = jnp.arange(0, QK_ROPE, 2, dtype=F32) / QK_ROPE
    inv = ROPE_THETA ** (-half)
    zeros = jnp.zeros((ROPE_PAD - QK_ROPE,), F32)
    inv_pad = jnp.concatenate([inv, inv, zeros]).reshape(1, ROPE_PAD)
    sign_pad = jnp.concatenate([-jnp.ones_like(inv), jnp.ones_like(inv), zeros]).reshape(1, ROPE_PAD)

    wgu_all = w_ffn_gate_up.astype(BF16)
    wd_all = w_ffn_down.astype(BF16)

    def ffn_args(l):
        return (row(norm_mix_post, l), row(norm_ffn_pre, l), wgu_all, wd_all, row(norm_ffn_post, l), l)

    rg = (row(norm_mix_pre, 0), rg_w_in[0].astype(BF16), rg_conv_w[0], row(rg_conv_b, 0),
          rg_gate_a_w[0].astype(BF16), row(rg_gate_a_b, 0), rg_gate_x_w[0].astype(BF16), row(rg_gate_x_b, 0),
          row(rg_lambda, 0))
    wo0 = rg_w_out[0].astype(BF16)
    yrg_p, ymem_p, p_h, p_conv = _rg_prompt(xp, *rg, mk_p[0], mv_p[0], tm=tm)
    xp = _out_ffn(xp, yrg_p, ymem_p, wo0, *ffn_args(0), tm=tm)

    prev_s = jnp.transpose(state_rg_conv[0], (1, 0, 2))
    yrg_s, mq_s, s_h, s_conv = _rg_sample(xs, *rg, prev_s, state_rg_h[0])
    ymem_s = _mem_sample(mq_s, cache_mem_k, cache_mem_v, 0, group=8)
    xs = _out_ffn(xs, yrg_s, ymem_s, wo0, *ffn_args(0), tm=bs)

    w_in_ext, wq_nope, wq_rope, wq_swap, wuk_t, wuk_flat, wuv_t = _mla_weights(
        mla_w_in[0], mla_w_q_up[0], mla_w_uk[0], mla_w_uv[0])
    mla_head = (row(norm_mix_pre, 1), w_in_ext, row(mla_q_norm, 0), row(mla_kv_norm, 0), wq_nope, wq_rope, wq_swap)
    wo1 = mla_w_out[0].astype(BF16)

    p_ckv, p_kpe, v_p, q_p, ymem_p, k_p = _mla_pre(xp, *mla_head, wuk_flat, inv_pad, sign_pad, (mk_p[1], mv_p[1]),
                                                   tm=tm, pos_base=0, pos_stride=1)
    yatt_p = _mla_prompt_attn(q_p, k_p, v_p, wuv_t, t=1024)
    xp = _out_ffn(xp, yatt_p, ymem_p, wo1, *ffn_args(1), tm=tm)

    s_ckv, s_kpe, kcat_s, qcat_s, mq_s = _mla_pre(xs, *mla_head, wuk_t, inv_pad, sign_pad, None,
                                                  tm=bs, pos_base=past, pos_stride=0)
    ymem_s = _mem_sample(mq_s, cache_mem_k, cache_mem_v, 1, group=8)
    kpe_pages = jnp.swapaxes(cache_kpe[0], 1, 2)
    o_lat = _mla_sample_attn(page_table, jnp.transpose(qcat_s, (1, 0, 2)), kcat_s, cache_ckv[0], kpe_pages,
                             chunk_pages=64)
    yatt_s = _uv_project(jnp.transpose(o_lat, (1, 0, 2)).astype(BF16), wuv_t)
    xs = _out_ffn(xs, yatt_s, ymem_s, wo1, *ffn_args(1), tm=bs)

    mem_shape = (depth, bp, n_mem, MEM_HEADS, MEM_HD)
    return (xp.reshape(bp, sp, d), xs.reshape(bs, ss, d),
            p_h.reshape(1, bp, d), p_conv.reshape(1, bp, CONV_W - 1, d),
            p_ckv.reshape(1, bp, sp, KV_LORA), p_kpe.reshape(1, bp, sp, QK_ROPE),
            mk_p.reshape(mem_shape), mv_p.reshape(mem_shape),
            s_h.reshape(1, bs, d), jnp.transpose(s_conv, (1, 0, 2)).reshape(1, bs, CONV_W - 1, d),
            s_ckv.reshape(1, bs, ss, KV_LORA), s_kpe.reshape(1, bs, ss, QK_ROPE))
```

```python
import functools

import jax
import jax.numpy as jnp
from jax import lax
from jax.experimental import pallas as pl
from jax.experimental.pallas import tpu as pltpu

F32 = jnp.float32
BF16 = jnp.bfloat16

EPS = 1e-6
RG_C = 8.0
RG_BLOCKS = 8
RG_BLOCK_W = 128
CONV_W = 4
MLA_HEADS = 8
Q_LORA = 512
KV_LORA = 256
QK_NOPE = 128
QK_ROPE = 64
V_HEAD = 128
ROPE_THETA = 10000.0
MLA_SCALE = (QK_NOPE + QK_ROPE) ** -0.5
MEM_HEADS = 4
MEM_HD = 128
MEM_W = MEM_HEADS * MEM_HD
MEM_SCALE = MEM_HD ** -0.5
PAGE_SIZE = 128

LANES = 128
SUBLANES = 8
ROPE_PAD = LANES
KCAT = KV_LORA + ROPE_PAD
VMEM_LIMIT = 56 * 1024 * 1024
VMEM_LIMIT_ATTN = 60 * 1024 * 1024
LOG2E = 1.4426950408889634
NEG_BIG = -1e30
FFN_CHUNK = 1024

NT_DIMS = (((1,), (1,)), ((), ()))
BATCH_NT = (((2,), (2,)), ((0,), (0,)))
BATCH_NN = (((2,), (1,)), ((0,), (0,)))


def _const_spec(shape):
    nd = len(shape)
    return pl.BlockSpec(shape, lambda *_: (0,) * nd, pipeline_mode=pl.Buffered(1))


def _params(n_axes=1):
    return pltpu.CompilerParams(dimension_semantics=("arbitrary",) * n_axes,
                                vmem_limit_bytes=VMEM_LIMIT)


def _rms(x, g):
    return x * lax.rsqrt(jnp.mean(x * x, axis=-1, keepdims=True) + EPS) * g


def _bdot(a, b):
    return jnp.dot(a.astype(BF16), b, preferred_element_type=F32)


def _gelu_tanh(x):
    return 0.5 * x * (1.0 + jnp.tanh(0.7978845608028654 * (x + 0.044715 * (x * x * x))))


def _softplus(x):
    return jnp.maximum(x, 0.0) + jnp.log1p(jnp.exp(-jnp.abs(x)))


def _shared_mem_attend(mq, mk, mv):
    outs = []
    for h in range(MEM_HEADS):
        hs = slice(h * MEM_HD, (h + 1) * MEM_HD)
        q = (mq[:, hs] * MEM_SCALE).astype(BF16)
        s = lax.dot_general(q, mk[:, hs].astype(BF16), NT_DIMS, preferred_element_type=F32)
        p = jnp.exp(s - jnp.max(s, axis=-1, keepdims=True))
        l = jnp.sum(p, axis=-1, keepdims=True)
        outs.append(_bdot(p, mv[:, hs].astype(BF16)) / l)
    return jnp.concatenate(outs, axis=-1)


def _memkv_kernel(mem_ref, g_ref, w_ref, mk_ref, mv_ref):
    kv = _bdot(_rms(mem_ref[...], g_ref[0]), w_ref[0])
    mk_ref[0] = kv[:, :MEM_W]
    mv_ref[0] = kv[:, MEM_W:]


def _memkv(mem, g, w):
    depth = w.shape[0]
    n_mem, d = mem.shape
    out = jax.ShapeDtypeStruct((depth, n_mem, MEM_W), F32)
    return pl.pallas_call(
        _memkv_kernel,
        grid=(depth,),
        in_specs=[_const_spec((n_mem, d)),
                  pl.BlockSpec((1, 1, d), lambda l: (l, 0, 0)),
                  pl.BlockSpec((1, d, 2 * MEM_W), lambda l: (l, 0, 0))],
        out_specs=[pl.BlockSpec((1, n_mem, MEM_W), lambda l: (l, 0, 0))] * 2,
        out_shape=[out, out],
        compiler_params=_params(),
        name="memkv",
    )(mem, g, w)


def _rg_gates(xc, wa_ref, ba, wx_ref, bx, sp, a_out, u_out):
    xcb = xc.astype(BF16)
    for n in range(RG_BLOCKS):
        bs = slice(n * RG_BLOCK_W, (n + 1) * RG_BLOCK_W)
        r = jax.nn.sigmoid(jnp.dot(xcb[:, bs], wa_ref[n], preferred_element_type=F32) + ba[:, bs])
        i = jax.nn.sigmoid(jnp.dot(xcb[:, bs], wx_ref[n], preferred_element_type=F32) + bx[:, bs])
        log_a = -RG_C * r * sp[:, bs]
        a = jnp.exp(log_a)
        a_out[:, bs] = a
        u_out[:, bs] = jnp.sqrt(-jnp.tanh(log_a) * (1.0 + a * a)) * (i * xc[:, bs])


def _rg_prompt_kernel(x_ref, g_ref, win_ref, cw_ref, cb_ref, wa_ref, ba_ref, wx_ref, bx_ref,
                      lam_ref, mk_ref, mv_ref,
                      yrg_ref, ymem_ref, hlast_ref, conv_ref,
                      xb_ref, a_ref, u_ref, h_ref, *, tm):
    d = a_ref.shape[1]
    hist = SUBLANES

    @pl.when(pl.program_id(0) == 0)
    def _():
        xb_ref[0:hist, :] = jnp.zeros((hist, d), F32)
        h_ref[...] = jnp.zeros_like(h_ref)

    z = _bdot(_rms(x_ref[...], g_ref[...]), win_ref[...])
    gate = z[:, :d]
    xb = z[:, d:2 * d]
    mq = z[:, 2 * d:]

    xb_ref[hist:hist + tm, :] = xb
    cw = cw_ref[...]
    xc = xb_ref[hist - 3:hist - 3 + tm, :] * cw[0:1]
    xc = xc + xb_ref[hist - 2:hist - 2 + tm, :] * cw[1:2]
    xc = xc + xb_ref[hist - 1:hist - 1 + tm, :] * cw[2:3]
    xc = xc + xb * cw[3:4]
    xc = xc + cb_ref[...]
    tail = xb_ref[tm + hist - 3:tm + hist, :]
    conv_ref[...] = tail
    xb_ref[hist - 3:hist, :] = tail

    sp = _softplus(-lam_ref[...])
    _rg_gates(xc, wa_ref, ba_ref[...], wx_ref, bx_ref[...], sp, a_ref, u_ref)

    row = lax.broadcasted_iota(jnp.int32, (SUBLANES, d), 0)

    def group(g, h):
        r0 = pl.multiple_of(g * SUBLANES, SUBLANES)
        a = a_ref[pl.ds(r0, SUBLANES), :]
        b = u_ref[pl.ds(r0, SUBLANES), :]
        for s in (1, 2, 4):
            keep = row >= s
            b = jnp.where(keep, a * pltpu.roll(b, s, axis=0) + b, b)
            a = jnp.where(keep, a * pltpu.roll(a, s, axis=0), a)
        hs = a * h + b
        u_ref[pl.ds(r0, SUBLANES), :] = hs
        return hs[SUBLANES - 1:SUBLANES, :]

    h_end = lax.fori_loop(0, tm // SUBLANES, group, h_ref[...])
    h_ref[...] = h_end
    hlast_ref[...] = h_end

    yrg_ref[...] = (u_ref[...] * _gelu_tanh(gate)).astype(BF16)
    ymem_ref[...] = _shared_mem_attend(mq, mk_ref[...], mv_ref[...]).astype(BF16)


def _rg_prompt(x, g, w_in, cw, cb, wa, ba, wx, bx, lam, mk, mv, *, tm):
    s, d = x.shape
    rg_in = w_in.shape[1]
    row = lambda i: (i, 0)
    return pl.pallas_call(
        functools.partial(_rg_prompt_kernel, tm=tm),
        grid=(s // tm,),
        in_specs=[pl.BlockSpec((tm, d), row),
                  _const_spec((1, d)), _const_spec((d, rg_in)),
                  _const_spec((CONV_W, d)), _const_spec((1, d)),
                  _const_spec(wa.shape), _const_spec((1, d)),
                  _const_spec(wx.shape), _const_spec((1, d)),
                  _const_spec((1, d)),
                  _const_spec(mk.shape), _const_spec(mv.shape)],
        out_specs=[pl.BlockSpec((tm, d), row),
                   pl.BlockSpec((tm, MEM_W), row),
                   pl.BlockSpec((1, d), lambda i: (0, 0)),
                   pl.BlockSpec((CONV_W - 1, d), lambda i: (0, 0))],
        out_shape=[jax.ShapeDtypeStruct((s, d), BF16),
                   jax.ShapeDtypeStruct((s, MEM_W), BF16),
                   jax.ShapeDtypeStruct((1, d), F32),
                   jax.ShapeDtypeStruct((CONV_W - 1, d), F32)],
        scratch_shapes=[pltpu.VMEM((tm + SUBLANES, d), F32),
                        pltpu.VMEM((tm, d), F32),
                        pltpu.VMEM((tm, d), F32),
                        pltpu.VMEM((1, d), F32)],
        compiler_params=_params(),
        name="rg_prompt",
    )(x, g, w_in, cw, cb, wa, ba, wx, bx, lam, mk, mv)


def _rg_sample_kernel(x_ref, g_ref, win_ref, cw_ref, cb_ref, wa_ref, ba_ref, wx_ref, bx_ref,
                      lam_ref, prev_ref, h0_ref,
                      yrg_ref, mq_ref, hnew_ref, conv_ref, a_ref, u_ref):
    d = a_ref.shape[1]
    z = _bdot(_rms(x_ref[...], g_ref[...]), win_ref[...])
    gate = z[:, :d]
    xb = z[:, d:2 * d]
    mq_ref[...] = z[:, 2 * d:]

    cw = cw_ref[...]
    xc = prev_ref[0] * cw[0:1]
    xc = xc + prev_ref[1] * cw[1:2]
    xc = xc + prev_ref[2] * cw[2:3]
    xc = xc + xb * cw[3:4]
    xc = xc + cb_ref[...]
    conv_ref[0] = prev_ref[1]
    conv_ref[1] = prev_ref[2]
    conv_ref[2] = xb

    sp = _softplus(-lam_ref[...])
    _rg_gates(xc, wa_ref, ba_ref[...], wx_ref, bx_ref[...], sp, a_ref, u_ref)
    h = u_ref[...] + a_ref[...] * h0_ref[...]
    hnew_ref[...] = h
    yrg_ref[...] = (h * _gelu_tanh(gate)).astype(BF16)


def _rg_sample(x, g, w_in, cw, cb, wa, ba, wx, bx, lam, prev, h0):
    b, d = x.shape
    args = (x, g, w_in, cw, cb, wa, ba, wx, bx, lam, prev, h0)
    return pl.pallas_call(
        _rg_sample_kernel,
        grid=(1,),
        in_specs=[_const_spec(a.shape) for a in args],
        out_specs=[_const_spec((b, d)), _const_spec((b, MEM_W)), _const_spec((b, d)),
                   _const_spec((CONV_W - 1, b, d))],
        out_shape=[jax.ShapeDtypeStruct((b, d), BF16),
                   jax.ShapeDtypeStruct((b, MEM_W), F32),
                   jax.ShapeDtypeStruct((b, d), F32),
                   jax.ShapeDtypeStruct((CONV_W - 1, b, d), F32)],
        scratch_shapes=[pltpu.VMEM((b, d), F32), pltpu.VMEM((b, d), F32)],
        compiler_params=_params(),
        name="rg_sample",
    )(*args)


MEM_Q_ROWS = 2 * SUBLANES


def _mem_sample_kernel(q_ref, k_ref, v_ref, o_ref):
    g, rows, _ = k_ref.shape
    q = (q_ref[...] * MEM_SCALE).astype(BF16)
    s = lax.dot_general(q, k_ref[...].astype(BF16), BATCH_NT, preferred_element_type=F32)
    q_head = lax.broadcasted_iota(jnp.int32, (g, MEM_Q_ROWS, rows), 1)
    k_head = lax.broadcasted_iota(jnp.int32, (g, MEM_Q_ROWS, rows), 2) & (MEM_HEADS - 1)
    s = jnp.where(q_head == k_head, s, NEG_BIG)
    p = jnp.exp(s - jnp.max(s, axis=-1, keepdims=True))
    l = jnp.sum(p, axis=-1, keepdims=True)
    o_ref[...] = lax.dot_general(p.astype(BF16), v_ref[...].astype(BF16), BATCH_NN,
                                 preferred_element_type=F32) / l


def _mem_sample(mq, cache_k, cache_v, layer, *, group):
    depth, b, n_mem, heads, hd = cache_k.shape
    rows = n_mem * heads
    k = cache_k.reshape(depth * b, rows, hd)
    v = cache_v.reshape(depth * b, rows, hd)
    q = jnp.pad(mq.reshape(b, heads, hd), ((0, 0), (0, MEM_Q_ROWS - heads), (0, 0)))
    base = layer * (b // group)
    out = pl.pallas_call(
        _mem_sample_kernel,
        grid=(b // group,),
        in_specs=[pl.BlockSpec((group, MEM_Q_ROWS, hd), lambda i: (i, 0, 0)),
                  pl.BlockSpec((group, rows, hd), lambda i: (base + i, 0, 0)),
                  pl.BlockSpec((group, rows, hd), lambda i: (base + i, 0, 0))],
        out_specs=pl.BlockSpec((group, MEM_Q_ROWS, hd), lambda i: (i, 0, 0)),
        out_shape=jax.ShapeDtypeStruct((b, MEM_Q_ROWS, hd), F32),
        compiler_params=_params(),
        name="mem_sample",
    )(q, k, v)
    return out[:, :heads, :].reshape(b, heads * hd)


def _mla_pre_kernel(x_ref, g_ref, win_ref, qn_ref, kvn_ref, wqn_ref, wqp_ref, wqs_ref, wuk_ref,
                    inv_ref, sign_ref, *rest, tm, pos_base, pos_stride, prompt):
    if prompt:
        mk_ref, mv_ref, ckv_ref, kpe_ref, kshared_ref, q_ref, mem_ref, kheads_ref = rest
    else:
        ckv_ref, kpe_ref, kshared_ref, q_ref, mem_ref = rest
    o_kv = Q_LORA
    o_mq = Q_LORA + KV_LORA
    o_pe = o_mq + MEM_W
    o_ps = o_pe + ROPE_PAD

    z = _bdot(_rms(x_ref[...], g_ref[...]), win_ref[...])
    cq = _rms(z[:, :o_kv], qn_ref[...])
    ckv = _rms(z[:, o_kv:o_mq], kvn_ref[...])
    mq = z[:, o_mq:o_pe]

    t = pl.program_id(0) * tm + lax.broadcasted_iota(jnp.int32, (tm, ROPE_PAD), 0)
    pos = (pos_base + pos_stride * t).astype(F32)
    ang = pos * inv_ref[...]
    cos = jnp.cos(ang)
    sin = jnp.sin(ang) * sign_ref[...]

    kpe = z[:, o_pe:o_ps] * cos + z[:, o_ps:] * sin
    ckv_ref[...] = ckv
    kpe_ref[...] = kpe[:, :QK_ROPE]
    ckv_b = ckv.astype(BF16)
    kpe_b = kpe.astype(BF16)

    cqb = cq.astype(BF16)
    q_nope = jnp.dot(cqb, wqn_ref[...], preferred_element_type=F32)
    q_rope = jnp.dot(cqb, wqp_ref[...], preferred_element_type=F32)
    q_swap = jnp.dot(cqb, wqs_ref[...], preferred_element_type=F32)
    if prompt:
        kshared_ref[...] = ckv_b
        k_nope = jnp.dot(ckv_b, wuk_ref[...], preferred_element_type=F32)
    else:
        kshared_ref[...] = jnp.concatenate([ckv_b, kpe_b], axis=-1)
    for h in range(MLA_HEADS):
        ns = slice(h * QK_NOPE, (h + 1) * QK_NOPE)
        ps = slice(h * ROPE_PAD, (h + 1) * ROPE_PAD)
        q_pe = q_rope[:, ps] * cos + q_swap[:, ps] * sin
        if prompt:
            q_main = q_nope[:, ns]
            kheads_ref[h] = jnp.concatenate([k_nope[:, ns].astype(BF16), kpe_b], axis=-1)
        else:
            q_main = _bdot(q_nope[:, ns], wuk_ref[h])
        q_scale = MLA_SCALE * LOG2E if prompt else MLA_SCALE
        q_ref[h] = (jnp.concatenate([q_main, q_pe], axis=-1) * q_scale).astype(BF16)

    if prompt:
        mem_ref[...] = _shared_mem_attend(mq, mk_ref[...], mv_ref[...]).astype(mem_ref.dtype)
    else:
        mem_ref[...] = mq


def _mla_pre(x, g, w_in, qn, kvn, wqn, wqp, wqs, wuk, inv, sign, mem_kv, *, tm, pos_base, pos_stride):
    s, d = x.shape
    prompt = mem_kv is not None
    row = lambda i: (i, 0)
    heads_row = lambda i: (0, i, 0)
    consts = [g, w_in, qn, kvn, wqn, wqp, wqs, wuk, inv, sign] + (list(mem_kv) if prompt else [])
    qw = QK_NOPE + ROPE_PAD if prompt else KCAT
    kw = KV_LORA if prompt else KCAT
    out_specs = [pl.BlockSpec((tm, KV_LORA), row),
                 pl.BlockSpec((tm, QK_ROPE), row),
                 pl.BlockSpec((tm, kw), row),
                 pl.BlockSpec((MLA_HEADS, tm, qw), heads_row),
                 pl.BlockSpec((tm, MEM_W), row)]
    out_shape = [jax.ShapeDtypeStruct((s, KV_LORA), F32),
                 jax.ShapeDtypeStruct((s, QK_ROPE), F32),
                 jax.ShapeDtypeStruct((s, kw), BF16),
                 jax.ShapeDtypeStruct((MLA_HEADS, s, qw), BF16),
                 jax.ShapeDtypeStruct((s, MEM_W), BF16 if prompt else F32)]
    if prompt:
        out_specs.append(pl.BlockSpec((MLA_HEADS, tm, qw), heads_row))
        out_shape.append(jax.ShapeDtypeStruct((MLA_HEADS, s, qw), BF16))
    return pl.pallas_call(
        functools.partial(_mla_pre_kernel, tm=tm, pos_base=pos_base, pos_stride=pos_stride, prompt=prompt),
        grid=(s // tm,),
        in_specs=[pl.BlockSpec((tm, d), row)] + [_const_spec(a.shape) for a in consts],
        out_specs=out_specs,
        out_shape=out_shape,
        compiler_params=_params(),
        name="mla_pre",
    )(x, *consts)


CACHE_SLOTS = 3
CACHE_LOOKAHEAD = 2


def _mla_attn_kernel(pt_ref, q_ref, k_ref, v_ref, wuv_ref, qs_ref, knew_ref, ckv_hbm, kpe_hbm,
                     y_ref, os_ref,
                     m_ref, l_ref, acc_ref, s_ref,
                     ckv_buf, kpe_buf, sems, ms_ref, ls_ref, accs_ref, cnt_ref, *, t, chunk_pages):
    h = pl.program_id(0)
    i = pl.program_id(1)
    n_seq, n_pages = pt_ref.shape
    n_c = n_pages // chunk_pages
    total = n_seq * n_c
    chunk = chunk_pages * PAGE_SIZE

    def copies(g, slot):
        b = g // n_c
        c = g % n_c
        out = []
        for p in range(chunk_pages):
            page = pt_ref[b, c * chunk_pages + p]
            rows = pl.ds(p * PAGE_SIZE, PAGE_SIZE)
            out.append(pltpu.make_async_copy(ckv_hbm.at[page], ckv_buf.at[slot, rows], sems.at[0, slot]))
            out.append(pltpu.make_async_copy(kpe_hbm.at[page], kpe_buf.at[slot, p], sems.at[1, slot]))
        return out

    def cache_fetch(g):
        for cp in copies(g, g % CACHE_SLOTS):
            cp.wait()
        nxt = g + CACHE_LOOKAHEAD
        for cp in copies(nxt % total, nxt % CACHE_SLOTS):
            cp.start()

    def cache_attend(g):
        b = g // n_c
        first = (g % n_c) == 0
        slot = g % CACHE_SLOTS
        q = qs_ref[b]
        q_pe = q[:, KV_LORA:KV_LORA + QK_ROPE]
        kc = ckv_buf[slot].astype(BF16)
        s = lax.dot_general(q[:, :KV_LORA], kc, NT_DIMS, preferred_element_type=F32)
        s = s + jnp.concatenate([_bdot(q_pe, kpe_buf[slot, p].astype(BF16)) for p in range(chunk_pages)], axis=-1)
        m_prev = jnp.where(first, NEG_BIG, ms_ref[...])
        l_prev = jnp.where(first, 0.0, ls_ref[...])
        acc_prev = jnp.where(first, 0.0, accs_ref[...])
        m_new = jnp.maximum(m_prev, jnp.max(s, axis=-1, keepdims=True))
        alpha = jnp.exp(m_prev - m_new)
        p = jnp.exp(s - jnp.tile(m_new, (1, chunk // LANES)))
        p_sum = p[:, :LANES]
        for c in range(1, chunk // LANES):
            p_sum = p_sum + p[:, c * LANES:(c + 1) * LANES]
        l = alpha * l_prev + p_sum
        acc = jnp.tile(alpha, (1, KV_LORA // LANES)) * acc_prev + _bdot(p, kc)
        ms_ref[...] = m_new
        ls_ref[...] = l
        accs_ref[...] = acc
        kn = knew_ref[pl.ds(b, 1), :]
        s_new = jnp.sum(q.astype(F32) * kn, axis=-1, keepdims=True)
        m_old = m_new[:, :1]
        m_all = jnp.maximum(m_old, s_new)
        scale_old = jnp.exp(m_old - m_all)
        p_new = jnp.exp(s_new - m_all)
        l_all = scale_old * jnp.sum(l, axis=-1, keepdims=True) + p_new
        os_ref[b] = (scale_old * acc + p_new.astype(BF16).astype(F32) * kn[:, :KV_LORA]) / l_all

    @pl.when((h == 0) & (i == 0))
    def _():
        cnt_ref[0] = 0
        for g in range(CACHE_LOOKAHEAD):
            for cp in copies(g, g % CACHE_SLOTS):
                cp.start()

    m_ref[...] = jnp.full_like(m_ref, NEG_BIG)
    l_ref[...] = jnp.zeros_like(l_ref)
    acc_ref[...] = jnp.zeros_like(acc_ref)

    def scores(k_start, slot):
        s_ref[slot] = lax.dot_general(q_ref[...], k_ref[pl.ds(k_start, t), :], NT_DIMS,
                                      preferred_element_type=F32)

    def update(k_start, slot, masked):
        s = s_ref[slot]
        if masked:
            q_pos = lax.broadcasted_iota(jnp.int32, (t, t), 0)
            k_pos = lax.broadcasted_iota(jnp.int32, (t, t), 1)
            s = jnp.where(k_pos <= q_pos, s, NEG_BIG)
        m_prev = m_ref[...]
        m_new = jnp.maximum(m_prev, jnp.max(s, axis=-1, keepdims=True))
        alpha = jnp.exp2(m_prev - m_new)
        p = jnp.exp2(s - jnp.tile(m_new, (1, t // LANES)))
        p_sum = p[:, :LANES]
        for c in range(1, t // LANES):
            p_sum = p_sum + p[:, c * LANES:(c + 1) * LANES]
        l_ref[...] = alpha * l_ref[...] + p_sum
        pv = _bdot(p, v_ref[pl.ds(k_start, t), :])
        acc_ref[...] = jnp.tile(alpha, (1, KV_LORA // LANES)) * acc_ref[...] + pv
        m_ref[...] = m_new

    scores(0, 0)
    g0 = cnt_ref[0]
    n_pairs = i // 2
    n_with = jnp.minimum(n_pairs, (total - g0) // 2)

    def pair_with_cache(jj, c):
        g = g0 + 2 * jj
        k0 = pl.multiple_of(2 * jj * t, t)
        cache_fetch(g)
        scores(k0 + t, 1)
        update(k0, 0, False)
        cache_attend(g)
        cache_fetch(g + 1)
        scores(k0 + 2 * t, 0)
        update(k0 + t, 1, False)
        cache_attend(g + 1)
        return c

    def pair(jj, c):
        k0 = pl.multiple_of(2 * jj * t, t)
        scores(k0 + t, 1)
        update(k0, 0, False)
        scores(k0 + 2 * t, 0)
        update(k0 + t, 1, False)
        return c

    lax.fori_loop(0, n_with, pair_with_cache, 0)
    lax.fori_loop(n_with, n_pairs, pair, 0)
    g1 = g0 + 2 * n_with
    cnt_ref[0] = g1

    @pl.when((g0 < total) & (g1 == total))
    def _():
        for g in range(total, total + CACHE_LOOKAHEAD):
            for cp in copies(g % total, g % CACHE_SLOTS):
                cp.wait()

    k_diag = pl.multiple_of(i * t, t)

    @pl.when(i % 2 == 0)
    def _():
        update(k_diag, 0, True)

    @pl.when(i % 2 == 1)
    def _():
        scores(k_diag, 1)
        update(k_diag - t, 0, False)
        update(k_diag, 1, True)

    o = acc_ref[...] / jnp.sum(l_ref[...], axis=-1, keepdims=True)
    y_ref[...] = _bdot(o, wuv_ref[...]).astype(y_ref.dtype)


def _mla_attn(q_heads, k_heads, v, wuv, page_table, qs, knew, cache_ckv, kpe_pages, *, t, chunk_pages):
    heads, s, w = q_heads.shape
    n_seq, n_pages = page_table.shape
    n_tiles = s // t
    total = n_seq * (n_pages // chunk_pages)
    assert total % 2 == 0 and 2 * heads * sum(i // 2 for i in range(n_tiles)) >= total
    chunk = chunk_pages * PAGE_SIZE
    grid_spec = pltpu.PrefetchScalarGridSpec(
        num_scalar_prefetch=1,
        grid=(heads, n_tiles),
        in_specs=[pl.BlockSpec((None, t, w), lambda h, i, pt: (h, i, 0)),
                  pl.BlockSpec((None, s, w), lambda h, i, pt: (h, 0, 0), pipeline_mode=pl.Buffered(1)),
                  _const_spec(v.shape),
                  pl.BlockSpec((None, KV_LORA, V_HEAD), lambda h, i, pt: (h, 0, 0)),
                  _const_spec(qs.shape), _const_spec(knew.shape),
                  pl.BlockSpec(memory_space=pl.ANY), pl.BlockSpec(memory_space=pl.ANY)],
        out_specs=[pl.BlockSpec((t, V_HEAD), lambda h, i, pt: (i, h)),
                   pl.BlockSpec((n_seq, heads, KV_LORA), lambda h, i, pt: (0, 0, 0))],
        scratch_shapes=[pltpu.VMEM((t, LANES), F32), pltpu.VMEM((t, LANES), F32),
                        pltpu.VMEM((t, KV_LORA), F32), pltpu.VMEM((2, t, t), F32),
                        pltpu.VMEM((CACHE_SLOTS, chunk, KV_LORA), F32),
                        pltpu.VMEM((CACHE_SLOTS, chunk_pages, QK_ROPE, PAGE_SIZE), F32),
                        pltpu.SemaphoreType.DMA((2, CACHE_SLOTS)),
                        pltpu.VMEM((heads, LANES), F32), pltpu.VMEM((heads, LANES), F32),
                        pltpu.VMEM((heads, KV_LORA), F32),
                        pltpu.SMEM((1,), jnp.int32)],
    )
    return pl.pallas_call(
        functools.partial(_mla_attn_kernel, t=t, chunk_pages=chunk_pages),
        grid_spec=grid_spec,
        out_shape=[jax.ShapeDtypeStruct((s, heads * V_HEAD), BF16),
                   jax.ShapeDtypeStruct((n_seq, heads, KV_LORA), F32)],
        compiler_params=pltpu.CompilerParams(dimension_semantics=("arbitrary", "arbitrary"),
                                             vmem_limit_bytes=VMEM_LIMIT_ATTN),
        name="mla_attn",
    )(page_table, q_heads, k_heads, v, wuv, qs, knew, cache_ckv, kpe_pages)


def _uv_kernel(o_ref, wuv_ref, y_ref):
    for h in range(MLA_HEADS):
        y_ref[:, h * V_HEAD:(h + 1) * V_HEAD] = _bdot(o_ref[h], wuv_ref[h]).astype(y_ref.dtype)


def _uv_project(o_hm, wuv):
    heads, b, _ = o_hm.shape
    return pl.pallas_call(
        _uv_kernel,
        grid=(1,),
        in_specs=[_const_spec(o_hm.shape), _const_spec(wuv.shape)],
        out_specs=_const_spec((b, heads * V_HEAD)),
        out_shape=jax.ShapeDtypeStruct((b, heads * V_HEAD), BF16),
        compiler_params=_params(),
        name="uv_project",
    )(o_hm, wuv)


def _out_ffn_kernel(x_ref, ya_ref, ym_ref, wo_ref, gpost_ref, gpre_ref, wgu_ref, wd_ref, gfpost_ref, o_ref):
    da = ya_ref.shape[1]
    dff = wd_ref.shape[0]
    mix = _bdot(ya_ref[...], wo_ref[:da, :]) + _bdot(ym_ref[...], wo_ref[da:, :])
    x1 = x_ref[...] + _rms(mix, gpost_ref[...])
    hb = _rms(x1, gpre_ref[...]).astype(BF16)
    ffn = None
    for c0 in range(0, dff, FFN_CHUNK):
        c1 = min(c0 + FFN_CHUNK, dff)
        gate = jnp.dot(hb, wgu_ref[:, c0:c1], preferred_element_type=F32)
        up = jnp.dot(hb, wgu_ref[:, dff + c0:dff + c1], preferred_element_type=F32)
        part = _bdot((gate * jax.nn.sigmoid(gate)) * up, wd_ref[c0:c1, :])
        ffn = part if ffn is None else ffn + part
    o_ref[...] = x1 + _rms(ffn, gfpost_ref[...])


def _layer_spec(stacked, layer):
    tail = stacked.shape[1:]
    return pl.BlockSpec((None,) + tail, lambda *_: (layer,) + (0,) * len(tail), pipeline_mode=pl.Buffered(1))


def _out_ffn(x, ya, ym, wo, gpost, gpre, wgu, wd, gfpost, layer, *, tm):
    s, d = x.shape
    row = lambda i: (i, 0)
    consts = (wo, gpost, gpre, wgu, wd, gfpost)
    const_specs = [_const_spec(wo.shape), _const_spec(gpost.shape), _const_spec(gpre.shape),
                   _layer_spec(wgu, layer), _layer_spec(wd, layer), _const_spec(gfpost.shape)]
    return pl.pallas_call(
        _out_ffn_kernel,
        grid=(s // tm,),
        in_specs=[pl.BlockSpec((tm, d), row),
                  pl.BlockSpec((tm, ya.shape[1]), row),
                  pl.BlockSpec((tm, ym.shape[1]), row)] + const_specs,
        out_specs=pl.BlockSpec((tm, d), row),
        out_shape=jax.ShapeDtypeStruct((s, d), F32),
        compiler_params=_params(),
        name="out_ffn",
    )(x, ya, ym, *consts)


def _pad_cols(w, width):
    return jnp.pad(w, ((0, 0), (0, width - w.shape[1])))


def _swap_halves(w):
    half = w.shape[-1] // 2
    return jnp.concatenate([w[..., half:], w[..., :half]], axis=-1)


def _mla_weights(w_in, w_q_up, w_uk, w_uv):
    o2, o3 = Q_LORA + KV_LORA, Q_LORA + KV_LORA + QK_ROPE
    w_pe = w_in[:, o2:o3]
    w_in_ext = jnp.concatenate(
        [w_in[:, :o2], w_in[:, o3:], _pad_cols(w_pe, ROPE_PAD), _pad_cols(_swap_halves(w_pe), ROPE_PAD)],
        axis=1).astype(BF16)
    wq_nope = w_q_up[:, :, :QK_NOPE].reshape(Q_LORA, MLA_HEADS * QK_NOPE).astype(BF16)
    wq_pe = w_q_up[:, :, QK_NOPE:]
    pad = ((0, 0), (0, 0), (0, ROPE_PAD - QK_ROPE))
    wq_rope = jnp.pad(wq_pe, pad).reshape(Q_LORA, MLA_HEADS * ROPE_PAD).astype(BF16)
    wq_swap = jnp.pad(_swap_halves(wq_pe), pad).reshape(Q_LORA, MLA_HEADS * ROPE_PAD).astype(BF16)
    wuk_t = jnp.transpose(w_uk, (1, 2, 0)).astype(BF16)
    wuk_flat = w_uk.reshape(KV_LORA, MLA_HEADS * QK_NOPE).astype(BF16)
    wuv_t = jnp.transpose(w_uv, (1, 0, 2)).astype(BF16)
    return w_in_ext, wq_nope, wq_rope, wq_swap, wuk_t, wuk_flat, wuv_t


def kernel(x_prompt, x_sample, mem_prompt, state_rg_h, state_rg_conv, cache_ckv, cache_kpe, cache_mem_k, cache_mem_v, page_table, norm_mix_pre, norm_mix_post, norm_ffn_pre, norm_ffn_post, norm_mem, w_mem_kv, w_ffn_gate_up, w_ffn_down, rg_w_in, rg_conv_w, rg_conv_b, rg_gate_a_w, rg_gate_a_b, rg_gate_x_w, rg_gate_x_b, rg_lambda, rg_w_out, mla_w_in, mla_q_norm, mla_kv_norm, mla_w_q_up, mla_w_uk, mla_w_uv, mla_w_out):
    bp, sp, d = x_prompt.shape
    bs, ss, _ = x_sample.shape
    depth = norm_mix_pre.shape[0]
    n_mem = mem_prompt.shape[1]
    assert bp == 1 and ss == 1 and depth == 2
    past = page_table.shape[1] * PAGE_SIZE
    tm = 512

    row = lambda a, l: a[l].reshape(1, -1)
    xp = x_prompt.reshape(sp, d)
    xs = x_sample.reshape(bs, d)

    mk_p, mv_p = _memkv(mem_prompt.reshape(n_mem, d), norm_mem.reshape(depth, 1, d), w_mem_kv.astype(BF16))

    half = jnp.arange(0, QK_ROPE, 2, dtype=F32) / QK_ROPE
    inv = ROPE_THETA ** (-half)
    zeros = jnp.zeros((ROPE_PAD - QK_ROPE,), F32)
    inv_pad = jnp.concatenate([inv, inv, zeros]).reshape(1, ROPE_PAD)
    sign_pad = jnp.concatenate([-jnp.ones_like(inv), jnp.ones_like(inv), zeros]).reshape(1, ROPE_PAD)

    wgu_all = w_ffn_gate_up.astype(BF16)
    wd_all = w_ffn_down.astype(BF16)

    def ffn_args(l):
        return (row(norm_mix_post, l), row(norm_ffn_pre, l), wgu_all, wd_all, row(norm_ffn_post, l), l)

    rg = (row(norm_mix_pre, 0), rg_w_in[0].astype(BF16), rg_conv_w[0], row(rg_conv_b, 0),
          rg_gate_a_w[0].astype(BF16), row(rg_gate_a_b, 0), rg_gate_x_w[0].astype(BF16), row(rg_gate_x_b, 0),
          row(rg_lambda, 0))
    wo0 = rg_w_out[0].astype(BF16)
    yrg_p, ymem_p, p_h, p_conv = _rg_prompt(xp, *rg, mk_p[0], mv_p[0], tm=tm)
    xp = _out_ffn(xp, yrg_p, ymem_p, wo0, *ffn_args(0), tm=tm)

    prev_s = jnp.transpose(state_rg_conv[0], (1, 0, 2))
    yrg_s, mq_s, s_h, s_conv = _rg_sample(xs, *rg, prev_s, state_rg_h[0])
    ymem_s = _mem_sample(mq_s, cache_mem_k, cache_mem_v, 0, group=8)
    xs = _out_ffn(xs, yrg_s, ymem_s, wo0, *ffn_args(0), tm=bs)

    w_in_ext, wq_nope, wq_rope, wq_swap, wuk_t, wuk_flat, wuv_t = _mla_weights(
        mla_w_in[0], mla_w_q_up[0], mla_w_uk[0], mla_w_uv[0])
    mla_head = (row(norm_mix_pre, 1), w_in_ext, row(mla_q_norm, 0), row(mla_kv_norm, 0), wq_nope, wq_rope, wq_swap)
    wo1 = mla_w_out[0].astype(BF16)

    s_ckv, s_kpe, kcat_s, qcat_s, mq_s = _mla_pre(xs, *mla_head, wuk_t, inv_pad, sign_pad, None,
                                                  tm=bs, pos_base=past, pos_stride=0)
    ymem_s = _mem_sample(mq_s, cache_mem_k, cache_mem_v, 1, group=8)
    p_ckv, p_kpe, v_p, q_p, ymem_p, k_p = _mla_pre(xp, *mla_head, wuk_flat, inv_pad, sign_pad, (mk_p[1], mv_p[1]),
                                                   tm=tm, pos_base=0, pos_stride=1)

    kpe_pages = jnp.swapaxes(cache_kpe[0], 1, 2)
    yatt_p, o_lat = _mla_attn(q_p, k_p, v_p, wuv_t, page_table, jnp.transpose(qcat_s, (1, 0, 2)),
                              kcat_s.astype(F32), cache_ckv[0], kpe_pages, t=1024, chunk_pages=32)
    xp = _out_ffn(xp, yatt_p, ymem_p, wo1, *ffn_args(1), tm=tm)
    yatt_s = _uv_project(jnp.transpose(o_lat, (1, 0, 2)).astype(BF16), wuv_t)
    xs = _out_ffn(xs, yatt_s, ymem_s, wo1, *ffn_args(1), tm=bs)

    mem_shape = (depth, bp, n_mem, MEM_HEADS, MEM_HD)
    return (xp.reshape(bp, sp, d), xs.reshape(bs, ss, d),
            p_h.reshape(1, bp, d), p_conv.reshape(1, bp, CONV_W - 1, d),
            p_ckv.reshape(1, bp, sp, KV_LORA), p_kpe.reshape(1, bp, sp, QK_ROPE),
            mk_p.reshape(mem_shape), mv_p.reshape(mem_shape),
            s_h.reshape(1, bs, d), jnp.transpose(s_conv, (1, 0, 2)).reshape(1, bs, CONV_W - 1, d),
            s_ckv.reshape(1, bs, ss, KV_LORA), s_kpe.reshape(1, bs, ss, QK_ROPE))
```

```python
import functools

import jax
import jax.numpy as jnp
from jax import lax
from jax.experimental import pallas as pl
from jax.experimental.pallas import tpu as pltpu

F32 = jnp.float32
BF16 = jnp.bfloat16

EPS = 1e-6
RG_C = 8.0
RG_BLOCKS = 8
RG_BLOCK_W = 128
CONV_W = 4
MLA_HEADS = 8
Q_LORA = 512
KV_LORA = 256
QK_NOPE = 128
QK_ROPE = 64
V_HEAD = 128
ROPE_THETA = 10000.0
MLA_SCALE = (QK_NOPE + QK_ROPE) ** -0.5
MEM_HEADS = 4
MEM_HD = 128
MEM_W = MEM_HEADS * MEM_HD
MEM_SCALE = MEM_HD ** -0.5
PAGE_SIZE = 128

LANES = 128
SUBLANES = 8
ROPE_PAD = LANES
KCAT = KV_LORA + ROPE_PAD
VMEM_LIMIT = 56 * 1024 * 1024
VMEM_LIMIT_ATTN = 60 * 1024 * 1024
LOG2E = 1.4426950408889634
NEG_BIG = -1e30
FFN_CHUNK = 1024

NT_DIMS = (((1,), (1,)), ((), ()))
BATCH_NT = (((2,), (2,)), ((0,), (0,)))
BATCH_NN = (((2,), (1,)), ((0,), (0,)))


def _const_spec(shape):
    nd = len(shape)
    return pl.BlockSpec(shape, lambda *_: (0,) * nd, pipeline_mode=pl.Buffered(1))


def _params(n_axes=1):
    return pltpu.CompilerParams(dimension_semantics=("arbitrary",) * n_axes,
                                vmem_limit_bytes=VMEM_LIMIT)


def _rms(x, g):
    return x * lax.rsqrt(jnp.mean(x * x, axis=-1, keepdims=True) + EPS) * g


def _bdot(a, b):
    return jnp.dot(a.astype(BF16), b, preferred_element_type=F32)


def _gelu_tanh(x):
    return 0.5 * x * (1.0 + jnp.tanh(0.7978845608028654 * (x + 0.044715 * (x * x * x))))


def _softplus(x):
    return jnp.maximum(x, 0.0) + jnp.log1p(jnp.exp(-jnp.abs(x)))


def _shared_mem_attend(mq, mk, mv):
    outs = []
    for h in range(MEM_HEADS):
        hs = slice(h * MEM_HD, (h + 1) * MEM_HD)
        q = (mq[:, hs] * MEM_SCALE).astype(BF16)
        s = lax.dot_general(q, mk[:, hs].astype(BF16), NT_DIMS, preferred_element_type=F32)
        p = jnp.exp(s - jnp.max(s, axis=-1, keepdims=True))
        l = jnp.sum(p, axis=-1, keepdims=True)
        outs.append(_bdot(p, mv[:, hs].astype(BF16)) / l)
    return jnp.concatenate(outs, axis=-1)


def _memkv_kernel(mem_ref, g_ref, w_ref, mk_ref, mv_ref):
    kv = _bdot(_rms(mem_ref[...], g_ref[0]), w_ref[0])
    mk_ref[0] = kv[:, :MEM_W]
    mv_ref[0] = kv[:, MEM_W:]


def _memkv(mem, g, w):
    depth = w.shape[0]
    n_mem, d = mem.shape
    out = jax.ShapeDtypeStruct((depth, n_mem, MEM_W), F32)
    return pl.pallas_call(
        _memkv_kernel,
        grid=(depth,),
        in_specs=[_const_spec((n_mem, d)),
                  pl.BlockSpec((1, 1, d), lambda l: (l, 0, 0)),
                  pl.BlockSpec((1, d, 2 * MEM_W), lambda l: (l, 0, 0))],
        out_specs=[pl.BlockSpec((1, n_mem, MEM_W), lambda l: (l, 0, 0))] * 2,
        out_shape=[out, out],
        compiler_params=_params(),
        name="memkv",
    )(mem, g, w)


def _rg_gates(xc, wa_ref, ba, wx_ref, bx, sp, a_out, u_out):
    xcb = xc.astype(BF16)
    for n in range(RG_BLOCKS):
        bs = slice(n * RG_BLOCK_W, (n + 1) * RG_BLOCK_W)
        r = jax.nn.sigmoid(jnp.dot(xcb[:, bs], wa_ref[n], preferred_element_type=F32) + ba[:, bs])
        i = jax.nn.sigmoid(jnp.dot(xcb[:, bs], wx_ref[n], preferred_element_type=F32) + bx[:, bs])
        log_a = -RG_C * r * sp[:, bs]
        a = jnp.exp(log_a)
        a_out[:, bs] = a
        u_out[:, bs] = jnp.sqrt(-jnp.tanh(log_a) * (1.0 + a * a)) * (i * xc[:, bs])


def _rg_prompt_kernel(x_ref, g_ref, win_ref, cw_ref, cb_ref, wa_ref, ba_ref, wx_ref, bx_ref,
                      lam_ref, mk_ref, mv_ref,
                      yrg_ref, ymem_ref, hlast_ref, conv_ref,
                      xb_ref, a_ref, u_ref, h_ref, *, tm):
    d = a_ref.shape[1]
    hist = SUBLANES

    @pl.when(pl.program_id(0) == 0)
    def _():
        xb_ref[0:hist, :] = jnp.zeros((hist, d), F32)
        h_ref[...] = jnp.zeros_like(h_ref)

    z = _bdot(_rms(x_ref[...], g_ref[...]), win_ref[...])
    gate = z[:, :d]
    xb = z[:, d:2 * d]
    mq = z[:, 2 * d:]

    xb_ref[hist:hist + tm, :] = xb
    cw = cw_ref[...]
    xc = xb_ref[hist - 3:hist - 3 + tm, :] * cw[0:1]
    xc = xc + xb_ref[hist - 2:hist - 2 + tm, :] * cw[1:2]
    xc = xc + xb_ref[hist - 1:hist - 1 + tm, :] * cw[2:3]
    xc = xc + xb * cw[3:4]
    xc = xc + cb_ref[...]
    tail = xb_ref[tm + hist - 3:tm + hist, :]
    conv_ref[...] = tail
    xb_ref[hist - 3:hist, :] = tail

    sp = _softplus(-lam_ref[...])
    _rg_gates(xc, wa_ref, ba_ref[...], wx_ref, bx_ref[...], sp, a_ref, u_ref)

    row = lax.broadcasted_iota(jnp.int32, (SUBLANES, d), 0)

    def group(g, h):
        r0 = pl.multiple_of(g * SUBLANES, SUBLANES)
        a = a_ref[pl.ds(r0, SUBLANES), :]
        b = u_ref[pl.ds(r0, SUBLANES), :]
        for s in (1, 2, 4):
            keep = row >= s
            b = jnp.where(keep, a * pltpu.roll(b, s, axis=0) + b, b)
            a = jnp.where(keep, a * pltpu.roll(a, s, axis=0), a)
        hs = a * h + b
        u_ref[pl.ds(r0, SUBLANES), :] = hs
        return hs[SUBLANES - 1:SUBLANES, :]

    h_end = lax.fori_loop(0, tm // SUBLANES, group, h_ref[...])
    h_ref[...] = h_end
    hlast_ref[...] = h_end

    yrg_ref[...] = (u_ref[...] * _gelu_tanh(gate)).astype(BF16)
    ymem_ref[...] = _shared_mem_attend(mq, mk_ref[...], mv_ref[...]).astype(BF16)


def _rg_prompt(x, g, w_in, cw, cb, wa, ba, wx, bx, lam, mk, mv, *, tm):
    s, d = x.shape
    rg_in = w_in.shape[1]
    row = lambda i: (i, 0)
    return pl.pallas_call(
        functools.partial(_rg_prompt_kernel, tm=tm),
        grid=(s // tm,),
        in_specs=[pl.BlockSpec((tm, d), row),
                  _const_spec((1, d)), _const_spec((d, rg_in)),
                  _const_spec((CONV_W, d)), _const_spec((1, d)),
                  _const_spec(wa.shape), _const_spec((1, d)),
                  _const_spec(wx.shape), _const_spec((1, d)),
                  _const_spec((1, d)),
                  _const_spec(mk.shape), _const_spec(mv.shape)],
        out_specs=[pl.BlockSpec((tm, d), row),
                   pl.BlockSpec((tm, MEM_W), row),
                   pl.BlockSpec((1, d), lambda i: (0, 0)),
                   pl.BlockSpec((CONV_W - 1, d), lambda i: (0, 0))],
        out_shape=[jax.ShapeDtypeStruct((s, d), BF16),
                   jax.ShapeDtypeStruct((s, MEM_W), BF16),
                   jax.ShapeDtypeStruct((1, d), F32),
                   jax.ShapeDtypeStruct((CONV_W - 1, d), F32)],
        scratch_shapes=[pltpu.VMEM((tm + SUBLANES, d), F32),
                        pltpu.VMEM((tm, d), F32),
                        pltpu.VMEM((tm, d), F32),
                        pltpu.VMEM((1, d), F32)],
        compiler_params=_params(),
        name="rg_prompt",
    )(x, g, w_in, cw, cb, wa, ba, wx, bx, lam, mk, mv)


def _rg_sample_kernel(x_ref, g_ref, win_ref, cw_ref, cb_ref, wa_ref, ba_ref, wx_ref, bx_ref,
                      lam_ref, prev_ref, h0_ref,
                      yrg_ref, mq_ref, hnew_ref, conv_ref, a_ref, u_ref):
    d = a_ref.shape[1]
    z = _bdot(_rms(x_ref[...], g_ref[...]), win_ref[...])
    gate = z[:, :d]
    xb = z[:, d:2 * d]
    mq_ref[...] = z[:, 2 * d:]

    cw = cw_ref[...]
    xc = prev_ref[0] * cw[0:1]
    xc = xc + prev_ref[1] * cw[1:2]
    xc = xc + prev_ref[2] * cw[2:3]
    xc = xc + xb * cw[3:4]
    xc = xc + cb_ref[...]
    conv_ref[0] = prev_ref[1]
    conv_ref[1] = prev_ref[2]
    conv_ref[2] = xb

    sp = _softplus(-lam_ref[...])
    _rg_gates(xc, wa_ref, ba_ref[...], wx_ref, bx_ref[...], sp, a_ref, u_ref)
    h = u_ref[...] + a_ref[...] * h0_ref[...]
    hnew_ref[...] = h
    yrg_ref[...] = (h * _gelu_tanh(gate)).astype(BF16)


def _rg_sample(x, g, w_in, cw, cb, wa, ba, wx, bx, lam, prev, h0):
    b, d = x.shape
    args = (x, g, w_in, cw, cb, wa, ba, wx, bx, lam, prev, h0)
    return pl.pallas_call(
        _rg_sample_kernel,
        grid=(1,),
        in_specs=[_const_spec(a.shape) for a in args],
        out_specs=[_const_spec((b, d)), _const_spec((b, MEM_W)), _const_spec((b, d)),
                   _const_spec((CONV_W - 1, b, d))],
        out_shape=[jax.ShapeDtypeStruct((b, d), BF16),
                   jax.ShapeDtypeStruct((b, MEM_W), F32),
                   jax.ShapeDtypeStruct((b, d), F32),
                   jax.ShapeDtypeStruct((CONV_W - 1, b, d), F32)],
        scratch_shapes=[pltpu.VMEM((b, d), F32), pltpu.VMEM((b, d), F32)],
        compiler_params=_params(),
        name="rg_sample",
    )(*args)


MEM_Q_ROWS = 2 * SUBLANES


def _mem_sample_kernel(q_ref, k_ref, v_ref, o_ref):
    g, rows, _ = k_ref.shape
    q = (q_ref[...] * MEM_SCALE).astype(BF16)
    s = lax.dot_general(q, k_ref[...].astype(BF16), BATCH_NT, preferred_element_type=F32)
    q_head = lax.broadcasted_iota(jnp.int32, (g, MEM_Q_ROWS, rows), 1)
    k_head = lax.broadcasted_iota(jnp.int32, (g, MEM_Q_ROWS, rows), 2) & (MEM_HEADS - 1)
    s = jnp.where(q_head == k_head, s, NEG_BIG)
    p = jnp.exp(s - jnp.max(s, axis=-1, keepdims=True))
    l = jnp.sum(p, axis=-1, keepdims=True)
    o_ref[...] = lax.dot_general(p.astype(BF16), v_ref[...].astype(BF16), BATCH_NN,
                                 preferred_element_type=F32) / l


def _mem_sample(mq, cache_k, cache_v, layer, *, group):
    depth, b, n_mem, heads, hd = cache_k.shape
    rows = n_mem * heads
    k = cache_k.reshape(depth * b, rows, hd)
    v = cache_v.reshape(depth * b, rows, hd)
    q = jnp.pad(mq.reshape(b, heads, hd), ((0, 0), (0, MEM_Q_ROWS - heads), (0, 0)))
    base = layer * (b // group)
    out = pl.pallas_call(
        _mem_sample_kernel,
        grid=(b // group,),
        in_specs=[pl.BlockSpec((group, MEM_Q_ROWS, hd), lambda i: (i, 0, 0)),
                  pl.BlockSpec((group, rows, hd), lambda i: (base + i, 0, 0)),
                  pl.BlockSpec((group, rows, hd), lambda i: (base + i, 0, 0))],
        out_specs=pl.BlockSpec((group, MEM_Q_ROWS, hd), lambda i: (i, 0, 0)),
        out_shape=jax.ShapeDtypeStruct((b, MEM_Q_ROWS, hd), F32),
        compiler_params=_params(),
        name="mem_sample",
    )(q, k, v)
    return out[:, :heads, :].reshape(b, heads * hd)


def _mla_pre_kernel(x_ref, g_ref, win_ref, qn_ref, kvn_ref, wqn_ref, wqp_ref, wqs_ref, wuk_ref,
                    inv_ref, sign_ref, *rest, tm, pos_base, pos_stride, prompt):
    if prompt:
        mk_ref, mv_ref, ckv_ref, kpe_ref, kshared_ref, q_ref, mem_ref, kheads_ref = rest
    else:
        ckv_ref, kpe_ref, kshared_ref, q_ref, mem_ref = rest
    o_kv = Q_LORA
    o_mq = Q_LORA + KV_LORA
    o_pe = o_mq + MEM_W
    o_ps = o_pe + ROPE_PAD

    z = _bdot(_rms(x_ref[...], g_ref[...]), win_ref[...])
    cq = _rms(z[:, :o_kv], qn_ref[...])
    ckv = _rms(z[:, o_kv:o_mq], kvn_ref[...])
    mq = z[:, o_mq:o_pe]

    t = pl.program_id(0) * tm + lax.broadcasted_iota(jnp.int32, (tm, ROPE_PAD), 0)
    pos = (pos_base + pos_stride * t).astype(F32)
    ang = pos * inv_ref[...]
    cos = jnp.cos(ang)
    sin = jnp.sin(ang) * sign_ref[...]

    kpe = z[:, o_pe:o_ps] * cos + z[:, o_ps:] * sin
    ckv_ref[...] = ckv
    kpe_ref[...] = kpe[:, :QK_ROPE]
    ckv_b = ckv.astype(BF16)
    kpe_b = kpe.astype(BF16)

    cqb = cq.astype(BF16)
    q_nope = jnp.dot(cqb, wqn_ref[...], preferred_element_type=F32)
    q_rope = jnp.dot(cqb, wqp_ref[...], preferred_element_type=F32)
    q_swap = jnp.dot(cqb, wqs_ref[...], preferred_element_type=F32)
    if prompt:
        kshared_ref[...] = ckv_b
        k_nope = jnp.dot(ckv_b, wuk_ref[...], preferred_element_type=F32)
    else:
        kshared_ref[...] = jnp.concatenate([ckv_b, kpe_b], axis=-1)
    for h in range(MLA_HEADS):
        ns = slice(h * QK_NOPE, (h + 1) * QK_NOPE)
        ps = slice(h * ROPE_PAD, (h + 1) * ROPE_PAD)
        q_pe = q_rope[:, ps] * cos + q_swap[:, ps] * sin
        if prompt:
            q_main = q_nope[:, ns]
            kheads_ref[h] = jnp.concatenate([k_nope[:, ns].astype(BF16), kpe_b], axis=-1)
        else:
            q_main = _bdot(q_nope[:, ns], wuk_ref[h])
        q_scale = MLA_SCALE * LOG2E if prompt else MLA_SCALE
        q_ref[h] = (jnp.concatenate([q_main, q_pe], axis=-1) * q_scale).astype(BF16)

    if prompt:
        mem_ref[...] = _shared_mem_attend(mq, mk_ref[...], mv_ref[...]).astype(mem_ref.dtype)
    else:
        mem_ref[...] = mq


def _mla_pre(x, g, w_in, qn, kvn, wqn, wqp, wqs, wuk, inv, sign, mem_kv, *, tm, pos_base, pos_stride):
    s, d = x.shape
    prompt = mem_kv is not None
    row = lambda i: (i, 0)
    heads_row = lambda i: (0, i, 0)
    consts = [g, w_in, qn, kvn, wqn, wqp, wqs, wuk, inv, sign] + (list(mem_kv) if prompt else [])
    qw = QK_NOPE + ROPE_PAD if prompt else KCAT
    kw = KV_LORA if prompt else KCAT
    out_specs = [pl.BlockSpec((tm, KV_LORA), row),
                 pl.BlockSpec((tm, QK_ROPE), row),
                 pl.BlockSpec((tm, kw), row),
                 pl.BlockSpec((MLA_HEADS, tm, qw), heads_row),
                 pl.BlockSpec((tm, MEM_W), row)]
    out_shape = [jax.ShapeDtypeStruct((s, KV_LORA), F32),
                 jax.ShapeDtypeStruct((s, QK_ROPE), F32),
                 jax.ShapeDtypeStruct((s, kw), BF16),
                 jax.ShapeDtypeStruct((MLA_HEADS, s, qw), BF16),
                 jax.ShapeDtypeStruct((s, MEM_W), BF16 if prompt else F32)]
    if prompt:
        out_specs.append(pl.BlockSpec((MLA_HEADS, tm, qw), heads_row))
        out_shape.append(jax.ShapeDtypeStruct((MLA_HEADS, s, qw), BF16))
    return pl.pallas_call(
        functools.partial(_mla_pre_kernel, tm=tm, pos_base=pos_base, pos_stride=pos_stride, prompt=prompt),
        grid=(s // tm,),
        in_specs=[pl.BlockSpec((tm, d), row)] + [_const_spec(a.shape) for a in consts],
        out_specs=out_specs,
        out_shape=out_shape,
        compiler_params=_params(),
        name="mla_pre",
    )(x, *consts)


def _mla_attn_kernel(pt_ref, q_ref, k_ref, v_ref, wuv_ref, qs_ref, knew_ref, ckv_hbm, kpe_hbm,
                     y_ref, os_ref,
                     m_ref, l_ref, acc_ref, s_ref,
                     ckv_buf, kpe_buf, sems, kc_ref, sc_ref, ms_ref, ls_ref, accs_ref, cnt_ref,
                     *, t, chunk_pages):
    h = pl.program_id(0)
    i = pl.program_id(1)
    n_seq, n_pages = pt_ref.shape
    n_c = n_pages // chunk_pages
    total = n_seq * n_c
    chunk = chunk_pages * PAGE_SIZE

    def copies(g, par):
        g = g % total
        b = g // n_c
        c = g % n_c
        out = []
        for p in range(chunk_pages):
            page = pt_ref[b, c * chunk_pages + p]
            rows = pl.ds(p * PAGE_SIZE, PAGE_SIZE)
            out.append(pltpu.make_async_copy(ckv_hbm.at[page], ckv_buf.at[par, rows], sems.at[0, par]))
            out.append(pltpu.make_async_copy(kpe_hbm.at[page], kpe_buf.at[par, p], sems.at[1, par]))
        return out

    def cache_start(g, par):
        for cp in copies(g, par):
            cp.start()

    def cache_wait(g, par):
        for cp in copies(g, par):
            cp.wait()

    def cache_scores(g, par):
        q = qs_ref[(g % total) // n_c]
        q_pe = q[:, KV_LORA:KV_LORA + QK_ROPE]
        kc_ref[par] = ckv_buf[par].astype(BF16)
        s = lax.dot_general(q[:, :KV_LORA], kc_ref[par], NT_DIMS, preferred_element_type=F32)
        sc_ref[par] = s + jnp.concatenate(
            [_bdot(q_pe, kpe_buf[par, p].astype(BF16)) for p in range(chunk_pages)], axis=-1)

    def cache_update(g, par):
        b = g // n_c
        first = (g % n_c) == 0
        q = qs_ref[b]
        s = sc_ref[par]
        m_prev = jnp.where(first, NEG_BIG, ms_ref[...])
        l_prev = jnp.where(first, 0.0, ls_ref[...])
        acc_prev = jnp.where(first, 0.0, accs_ref[...])
        m_new = jnp.maximum(m_prev, jnp.max(s, axis=-1, keepdims=True))
        alpha = jnp.exp(m_prev - m_new)
        p = jnp.exp(s - jnp.tile(m_new, (1, chunk // LANES)))
        p_sum = p[:, :LANES]
        for c in range(1, chunk // LANES):
            p_sum = p_sum + p[:, c * LANES:(c + 1) * LANES]
        l = alpha * l_prev + p_sum
        acc = jnp.tile(alpha, (1, KV_LORA // LANES)) * acc_prev + _bdot(p, kc_ref[par])
        ms_ref[...] = m_new
        ls_ref[...] = l
        accs_ref[...] = acc
        kn = knew_ref[pl.ds(b, 1), :]
        s_new = jnp.sum(q.astype(F32) * kn, axis=-1, keepdims=True)
        m_old = m_new[:, :1]
        m_all = jnp.maximum(m_old, s_new)
        scale_old = jnp.exp(m_old - m_all)
        p_new = jnp.exp(s_new - m_all)
        l_all = scale_old * jnp.sum(l, axis=-1, keepdims=True) + p_new
        os_ref[b] = (scale_old * acc + p_new.astype(BF16).astype(F32) * kn[:, :KV_LORA]) / l_all

    @pl.when((h == 0) & (i == 0))
    def _():
        cnt_ref[0] = 0
        cache_start(0, 0)
        cache_start(1, 1)
        cache_wait(0, 0)
        cache_scores(0, 0)
        cache_start(2, 0)

    m_ref[...] = jnp.full_like(m_ref, NEG_BIG)
    l_ref[...] = jnp.zeros_like(l_ref)
    acc_ref[...] = jnp.zeros_like(acc_ref)

    def scores(k_start, slot):
        s_ref[slot] = lax.dot_general(q_ref[...], k_ref[pl.ds(k_start, t), :], NT_DIMS,
                                      preferred_element_type=F32)

    def update(k_start, slot, masked):
        s = s_ref[slot]
        if masked:
            q_pos = lax.broadcasted_iota(jnp.int32, (t, t), 0)
            k_pos = lax.broadcasted_iota(jnp.int32, (t, t), 1)
            s = jnp.where(k_pos <= q_pos, s, NEG_BIG)
        m_prev = m_ref[...]
        m_new = jnp.maximum(m_prev, jnp.max(s, axis=-1, keepdims=True))
        alpha = jnp.exp2(m_prev - m_new)
        p = jnp.exp2(s - jnp.tile(m_new, (1, t // LANES)))
        p_sum = p[:, :LANES]
        for c in range(1, t // LANES):
            p_sum = p_sum + p[:, c * LANES:(c + 1) * LANES]
        l_ref[...] = alpha * l_ref[...] + p_sum
        pv = _bdot(p, v_ref[pl.ds(k_start, t), :])
        acc_ref[...] = jnp.tile(alpha, (1, KV_LORA // LANES)) * acc_ref[...] + pv
        m_ref[...] = m_new

    scores(0, 0)
    g0 = cnt_ref[0]
    n_pairs = i // 2
    n_with = jnp.minimum(n_pairs, (total - g0) // 2)

    def pair_with_cache(jj, c):
        g = g0 + 2 * jj
        k0 = pl.multiple_of(2 * jj * t, t)
        cache_wait(g + 1, 1)
        scores(k0 + t, 1)
        update(k0, 0, False)
        cache_update(g, 0)
        cache_scores(g + 1, 1)
        cache_start(g + 3, 1)
        cache_wait(g + 2, 0)
        scores(k0 + 2 * t, 0)
        update(k0 + t, 1, False)
        cache_update(g + 1, 1)
        cache_scores(g + 2, 0)
        cache_start(g + 4, 0)
        return c

    def pair(jj, c):
        k0 = pl.multiple_of(2 * jj * t, t)
        scores(k0 + t, 1)
        update(k0, 0, False)
        scores(k0 + 2 * t, 0)
        update(k0 + t, 1, False)
        return c

    lax.fori_loop(0, n_with, pair_with_cache, 0)
    lax.fori_loop(n_with, n_pairs, pair, 0)
    g1 = g0 + 2 * n_with
    cnt_ref[0] = g1

    @pl.when((g0 < total) & (g1 == total))
    def _():
        cache_wait(total + 1, 1)
        cache_wait(total + 2, 0)

    k_diag = pl.multiple_of(i * t, t)

    @pl.when(i % 2 == 0)
    def _():
        update(k_diag, 0, True)

    @pl.when(i % 2 == 1)
    def _():
        scores(k_diag, 1)
        update(k_diag - t, 0, False)
        update(k_diag, 1, True)

    o = acc_ref[...] / jnp.sum(l_ref[...], axis=-1, keepdims=True)
    y_ref[...] = _bdot(o, wuv_ref[...]).astype(y_ref.dtype)


def _mla_attn(q_heads, k_heads, v, wuv, page_table, qs, knew, cache_ckv, kpe_pages, *, t, chunk_pages):
    heads, s, w = q_heads.shape
    n_seq, n_pages = page_table.shape
    n_tiles = s // t
    total = n_seq * (n_pages // chunk_pages)
    assert total % 2 == 0 and 2 * heads * sum(i // 2 for i in range(n_tiles)) >= total
    chunk = chunk_pages * PAGE_SIZE
    grid_spec = pltpu.PrefetchScalarGridSpec(
        num_scalar_prefetch=1,
        grid=(heads, n_tiles),
        in_specs=[pl.BlockSpec((None, t, w), lambda h, i, pt: (h, i, 0)),
                  pl.BlockSpec((None, s, w), lambda h, i, pt: (h, 0, 0)),
                  _const_spec(v.shape),
                  pl.BlockSpec((None, KV_LORA, V_HEAD), lambda h, i, pt: (h, 0, 0)),
                  _const_spec(qs.shape), _const_spec(knew.shape),
                  pl.BlockSpec(memory_space=pl.ANY), pl.BlockSpec(memory_space=pl.ANY)],
        out_specs=[pl.BlockSpec((t, V_HEAD), lambda h, i, pt: (i, h)),
                   pl.BlockSpec((n_seq, heads, KV_LORA), lambda h, i, pt: (0, 0, 0))],
        scratch_shapes=[pltpu.VMEM((t, LANES), F32), pltpu.VMEM((t, LANES), F32),
                        pltpu.VMEM((t, KV_LORA), F32), pltpu.VMEM((2, t, t), F32),
                        pltpu.VMEM((2, chunk, KV_LORA), F32),
                        pltpu.VMEM((2, chunk_pages, QK_ROPE, PAGE_SIZE), F32),
                        pltpu.SemaphoreType.DMA((2, 2)),
                        pltpu.VMEM((2, chunk, KV_LORA), BF16), pltpu.VMEM((2, heads, chunk), F32),
                        pltpu.VMEM((heads, LANES), F32), pltpu.VMEM((heads, LANES), F32),
                        pltpu.VMEM((heads, KV_LORA), F32),
                        pltpu.SMEM((1,), jnp.int32)],
    )
    return pl.pallas_call(
        functools.partial(_mla_attn_kernel, t=t, chunk_pages=chunk_pages),
        grid_spec=grid_spec,
        out_shape=[jax.ShapeDtypeStruct((s, heads * V_HEAD), BF16),
                   jax.ShapeDtypeStruct((n_seq, heads, KV_LORA), F32)],
        compiler_params=pltpu.CompilerParams(dimension_semantics=("arbitrary", "arbitrary"),
                                             vmem_limit_bytes=VMEM_LIMIT_ATTN),
        name="mla_attn",
    )(page_table, q_heads, k_heads, v, wuv, qs, knew, cache_ckv, kpe_pages)


def _uv_kernel(o_ref, wuv_ref, y_ref):
    for h in range(MLA_HEADS):
        y_ref[:, h * V_HEAD:(h + 1) * V_HEAD] = _bdot(o_ref[h], wuv_ref[h]).astype(y_ref.dtype)


def _uv_project(o_hm, wuv):
    heads, b, _ = o_hm.shape
    return pl.pallas_call(
        _uv_kernel,
        grid=(1,),
        in_specs=[_const_spec(o_hm.shape), _const_spec(wuv.shape)],
        out_specs=_const_spec((b, heads * V_HEAD)),
        out_shape=jax.ShapeDtypeStruct((b, heads * V_HEAD), BF16),
        compiler_params=_params(),
        name="uv_project",
    )(o_hm, wuv)


def _out_ffn_kernel(x_ref, ya_ref, ym_ref, wo_ref, gpost_ref, gpre_ref, wgu_ref, wd_ref, gfpost_ref, o_ref):
    da = ya_ref.shape[1]
    dff = wd_ref.shape[0]
    mix = _bdot(ya_ref[...], wo_ref[:da, :]) + _bdot(ym_ref[...], wo_ref[da:, :])
    x1 = x_ref[...] + _rms(mix, gpost_ref[...])
    hb = _rms(x1, gpre_ref[...]).astype(BF16)
    ffn = None
    for c0 in range(0, dff, FFN_CHUNK):
        c1 = min(c0 + FFN_CHUNK, dff)
        gate = jnp.dot(hb, wgu_ref[:, c0:c1], preferred_element_type=F32)
        up = jnp.dot(hb, wgu_ref[:, dff + c0:dff + c1], preferred_element_type=F32)
        part = _bdot((gate * jax.nn.sigmoid(gate)) * up, wd_ref[c0:c1, :])
        ffn = part if ffn is None else ffn + part
    o_ref[...] = x1 + _rms(ffn, gfpost_ref[...])


def _layer_spec(stacked, layer):
    tail = stacked.shape[1:]
    return pl.BlockSpec((None,) + tail, lambda *_: (layer,) + (0,) * len(tail), pipeline_mode=pl.Buffered(1))


def _out_ffn(x, ya, ym, wo, gpost, gpre, wgu, wd, gfpost, layer, *, tm):
    s, d = x.shape
    row = lambda i: (i, 0)
    consts = (wo, gpost, gpre, wgu, wd, gfpost)
    const_specs = [_const_spec(wo.shape), _const_spec(gpost.shape), _const_spec(gpre.shape),
                   _layer_spec(wgu, layer), _layer_spec(wd, layer), _const_spec(gfpost.shape)]
    return pl.pallas_call(
        _out_ffn_kernel,
        grid=(s // tm,),
        in_specs=[pl.BlockSpec((tm, d), row),
                  pl.BlockSpec((tm, ya.shape[1]), row),
                  pl.BlockSpec((tm, ym.shape[1]), row)] + const_specs,
        out_specs=pl.BlockSpec((tm, d), row),
        out_shape=jax.ShapeDtypeStruct((s, d), F32),
        compiler_params=_params(),
        name="out_ffn",
    )(x, ya, ym, *consts)


def _pad_cols(w, width):
    return jnp.pad(w, ((0, 0), (0, width - w.shape[1])))


def _swap_halves(w):
    half = w.shape[-1] // 2
    return jnp.concatenate([w[..., half:], w[..., :half]], axis=-1)


def _mla_weights(w_in, w_q_up, w_uk, w_uv):
    o2, o3 = Q_LORA + KV_LORA, Q_LORA + KV_LORA + QK_ROPE
    w_pe = w_in[:, o2:o3]
    w_in_ext = jnp.concatenate(
        [w_in[:, :o2], w_in[:, o3:], _pad_cols(w_pe, ROPE_PAD), _pad_cols(_swap_halves(w_pe), ROPE_PAD)],
        axis=1).astype(BF16)
    wq_nope = w_q_up[:, :, :QK_NOPE].reshape(Q_LORA, MLA_HEADS * QK_NOPE).astype(BF16)
    wq_pe = w_q_up[:, :, QK_NOPE:]
    pad = ((0, 0), (0, 0), (0, ROPE_PAD - QK_ROPE))
    wq_rope = jnp.pad(wq_pe, pad).reshape(Q_LORA, MLA_HEADS * ROPE_PAD).astype(BF16)
    wq_swap = jnp.pad(_swap_halves(wq_pe), pad).reshape(Q_LORA, MLA_HEADS * ROPE_PAD).astype(BF16)
    wuk_t = jnp.transpose(w_uk, (1, 2, 0)).astype(BF16)
    wuk_flat = w_uk.reshape(KV_LORA, MLA_HEADS * QK_NOPE).astype(BF16)
    wuv_t = jnp.transpose(w_uv, (1, 0, 2)).astype(BF16)
    return w_in_ext, wq_nope, wq_rope, wq_swap, wuk_t, wuk_flat, wuv_t


def kernel(x_prompt, x_sample, mem_prompt, state_rg_h, state_rg_conv, cache_ckv, cache_kpe, cache_mem_k, cache_mem_v, page_table, norm_mix_pre, norm_mix_post, norm_ffn_pre, norm_ffn_post, norm_mem, w_mem_kv, w_ffn_gate_up, w_ffn_down, rg_w_in, rg_conv_w, rg_conv_b, rg_gate_a_w, rg_gate_a_b, rg_gate_x_w, rg_gate_x_b, rg_lambda, rg_w_out, mla_w_in, mla_q_norm, mla_kv_norm, mla_w_q_up, mla_w_uk, mla_w_uv, mla_w_out):
    bp, sp, d = x_prompt.shape
    bs, ss, _ = x_sample.shape
    depth = norm_mix_pre.shape[0]
    n_mem = mem_prompt.shape[1]
    assert bp == 1 and ss == 1 and depth == 2
    past = page_table.shape[1] * PAGE_SIZE
    tm = 512

    row = lambda a, l: a[l].reshape(1, -1)
    xp = x_prompt.reshape(sp, d)
    xs = x_sample.reshape(bs, d)

    mk_p, mv_p = _memkv(mem_prompt.reshape(n_mem, d), norm_mem.reshape(depth, 1, d), w_mem_kv.astype(BF16))

    half = jnp.arange(0, QK_ROPE, 2, dtype=F32) / QK_ROPE
    inv = ROPE_THETA ** (-half)
    zeros = jnp.zeros((ROPE_PAD - QK_ROPE,), F32)
    inv_pad = jnp.concatenate([inv, inv, zeros]).reshape(1, ROPE_PAD)
    sign_pad = jnp.concatenate([-jnp.ones_like(inv), jnp.ones_like(inv), zeros]).reshape(1, ROPE_PAD)

    wgu_all = w_ffn_gate_up.astype(BF16)
    wd_all = w_ffn_down.astype(BF16)

    def ffn_args(l):
        return (row(norm_mix_post, l), row(norm_ffn_pre, l), wgu_all, wd_all, row(norm_ffn_post, l), l)

    rg = (row(norm_mix_pre, 0), rg_w_in[0].astype(BF16), rg_conv_w[0], row(rg_conv_b, 0),
          rg_gate_a_w[0].astype(BF16), row(rg_gate_a_b, 0), rg_gate_x_w[0].astype(BF16), row(rg_gate_x_b, 0),
          row(rg_lambda, 0))
    wo0 = rg_w_out[0].astype(BF16)
    yrg_p, ymem_p, p_h, p_conv = _rg_prompt(xp, *rg, mk_p[0], mv_p[0], tm=tm)
    xp = _out_ffn(xp, yrg_p, ymem_p, wo0, *ffn_args(0), tm=tm)

    prev_s = jnp.transpose(state_rg_conv[0], (1, 0, 2))
    yrg_s, mq_s, s_h, s_conv = _rg_sample(xs, *rg, prev_s, state_rg_h[0])
    ymem_s = _mem_sample(mq_s, cache_mem_k, cache_mem_v, 0, group=8)
    xs = _out_ffn(xs, yrg_s, ymem_s, wo0, *ffn_args(0), tm=bs)

    w_in_ext, wq_nope, wq_rope, wq_swap, wuk_t, wuk_flat, wuv_t = _mla_weights(
        mla_w_in[0], mla_w_q_up[0], mla_w_uk[0], mla_w_uv[0])
    mla_head = (row(norm_mix_pre, 1), w_in_ext, row(mla_q_norm, 0), row(mla_kv_norm, 0), wq_nope, wq_rope, wq_swap)
    wo1 = mla_w_out[0].astype(BF16)

    s_ckv, s_kpe, kcat_s, qcat_s, mq_s = _mla_pre(xs, *mla_head, wuk_t, inv_pad, sign_pad, None,
                                                  tm=bs, pos_base=past, pos_stride=0)
    ymem_s = _mem_sample(mq_s, cache_mem_k, cache_mem_v, 1, group=8)
    p_ckv, p_kpe, v_p, q_p, ymem_p, k_p = _mla_pre(xp, *mla_head, wuk_flat, inv_pad, sign_pad, (mk_p[1], mv_p[1]),
                                                   tm=tm, pos_base=0, pos_stride=1)

    kpe_pages = jnp.swapaxes(cache_kpe[0], 1, 2)
    yatt_p, o_lat = _mla_attn(q_p, k_p, v_p, wuv_t, page_table, jnp.transpose(qcat_s, (1, 0, 2)),
                              kcat_s.astype(F32), cache_ckv[0], kpe_pages, t=1024, chunk_pages=32)
    xp = _out_ffn(xp, yatt_p, ymem_p, wo1, *ffn_args(1), tm=tm)
    yatt_s = _uv_project(jnp.transpose(o_lat, (1, 0, 2)).astype(BF16), wuv_t)
    xs = _out_ffn(xs, yatt_s, ymem_s, wo1, *ffn_args(1), tm=bs)

    mem_shape = (depth, bp, n_mem, MEM_HEADS, MEM_HD)
    return (xp.reshape(bp, sp, d), xs.reshape(bs, ss, d),
            p_h.reshape(1, bp, d), p_conv.reshape(1, bp, CONV_W - 1, d),
            p_ckv.reshape(1, bp, sp, KV_LORA), p_kpe.reshape(1, bp, sp, QK_ROPE),
            mk_p.reshape(mem_shape), mv_p.reshape(mem_shape),
            s_h.reshape(1, bs, d), jnp.transpose(s_conv, (1, 0, 2)).reshape(1, bs, CONV_W - 1, d),
            s_ckv.reshape(1, bs, ss, KV_LORA), s_kpe.reshape(1, bs, ss, QK_ROPE))
```

```python
import functools

import jax
import jax.numpy as jnp
from jax import lax
from jax.experimental import pallas as pl
from jax.experimental.pallas import tpu as pltpu

F32 = jnp.float32
BF16 = jnp.bfloat16

EPS = 1e-6
RG_C = 8.0
RG_BLOCKS = 8
RG_BLOCK_W = 128
CONV_W = 4
MLA_HEADS = 8
Q_LORA = 512
KV_LORA = 256
QK_NOPE = 128
QK_ROPE = 64
V_HEAD = 128
ROPE_THETA = 10000.0
MLA_SCALE = (QK_NOPE + QK_ROPE) ** -0.5
MEM_HEADS = 4
MEM_HD = 128
MEM_W = MEM_HEADS * MEM_HD
MEM_SCALE = MEM_HD ** -0.5
PAGE_SIZE = 128

LANES = 128
SUBLANES = 8
ROPE_PAD = LANES
KCAT = KV_LORA + ROPE_PAD
VMEM_LIMIT = 56 * 1024 * 1024
VMEM_LIMIT_ATTN = 60 * 1024 * 1024
LOG2E = 1.4426950408889634
NEG_BIG = -1e30
FFN_CHUNK = 1024
FFN_ROW_TILE = 512
MIXER_ROW_TILE = 1024
ATTN_TILE = 1024
CACHE_CHUNK_PAGES = 32
MEM_GROUP = 8

NT_DIMS = (((1,), (1,)), ((), ()))
BATCH_NT = (((2,), (2,)), ((0,), (0,)))
BATCH_NN = (((2,), (1,)), ((0,), (0,)))


def _const_spec(shape):
    nd = len(shape)
    return pl.BlockSpec(shape, lambda *_: (0,) * nd, pipeline_mode=pl.Buffered(1))


def _params(n_axes=1):
    return pltpu.CompilerParams(dimension_semantics=("arbitrary",) * n_axes,
                                vmem_limit_bytes=VMEM_LIMIT)


def _rms(x, g):
    return x * lax.rsqrt(jnp.mean(x * x, axis=-1, keepdims=True) + EPS) * g


def _bdot(a, b):
    return jnp.dot(a.astype(BF16), b, preferred_element_type=F32)


def _gelu_tanh(x):
    return 0.5 * x * (1.0 + jnp.tanh(0.7978845608028654 * (x + 0.044715 * (x * x * x))))


def _softplus(x):
    return jnp.maximum(x, 0.0) + jnp.log1p(jnp.exp(-jnp.abs(x)))


def _shared_mem_attend(mq, mk, mv):
    outs = []
    for h in range(MEM_HEADS):
        hs = slice(h * MEM_HD, (h + 1) * MEM_HD)
        q = (mq[:, hs] * MEM_SCALE).astype(BF16)
        s = lax.dot_general(q, mk[:, hs].astype(BF16), NT_DIMS, preferred_element_type=F32)
        p = jnp.exp(s - jnp.max(s, axis=-1, keepdims=True))
        l = jnp.sum(p, axis=-1, keepdims=True)
        outs.append(_bdot(p, mv[:, hs].astype(BF16)) / l)
    return jnp.concatenate(outs, axis=-1)


def _memkv_kernel(mem_ref, g_ref, w_ref, mk_ref, mv_ref):
    kv = _bdot(_rms(mem_ref[...], g_ref[0]), w_ref[0])
    mk_ref[0] = kv[:, :MEM_W]
    mv_ref[0] = kv[:, MEM_W:]


def _memkv(mem, g, w):
    depth = w.shape[0]
    n_mem, d = mem.shape
    out = jax.ShapeDtypeStruct((depth, n_mem, MEM_W), F32)
    return pl.pallas_call(
        _memkv_kernel,
        grid=(depth,),
        in_specs=[_const_spec((n_mem, d)),
                  pl.BlockSpec((1, 1, d), lambda l: (l, 0, 0)),
                  pl.BlockSpec((1, d, 2 * MEM_W), lambda l: (l, 0, 0))],
        out_specs=[pl.BlockSpec((1, n_mem, MEM_W), lambda l: (l, 0, 0))] * 2,
        out_shape=[out, out],
        compiler_params=_params(),
        name="memkv",
    )(mem, g, w)


def _rg_gates(xc, wa_ref, ba, wx_ref, bx, sp, a_out, u_out):
    xcb = xc.astype(BF16)
    for n in range(RG_BLOCKS):
        bs = slice(n * RG_BLOCK_W, (n + 1) * RG_BLOCK_W)
        r = jax.nn.sigmoid(jnp.dot(xcb[:, bs], wa_ref[n], preferred_element_type=F32) + ba[:, bs])
        i = jax.nn.sigmoid(jnp.dot(xcb[:, bs], wx_ref[n], preferred_element_type=F32) + bx[:, bs])
        log_a = -RG_C * r * sp[:, bs]
        a = jnp.exp(log_a)
        a_out[:, bs] = a
        u_out[:, bs] = jnp.sqrt(-jnp.tanh(log_a) * (1.0 + a * a)) * (i * xc[:, bs])


def _rg_prompt_kernel(x_ref, g_ref, win_ref, cw_ref, cb_ref, wa_ref, ba_ref, wx_ref, bx_ref,
                      lam_ref, mk_ref, mv_ref,
                      yrg_ref, ymem_ref, hlast_ref, conv_ref,
                      xb_ref, a_ref, u_ref, h_ref, *, tm):
    d = a_ref.shape[1]
    hist = SUBLANES

    @pl.when(pl.program_id(0) == 0)
    def _():
        xb_ref[0:hist, :] = jnp.zeros((hist, d), F32)
        h_ref[...] = jnp.zeros_like(h_ref)

    z = _bdot(_rms(x_ref[...], g_ref[...]), win_ref[...])
    gate = z[:, :d]
    xb = z[:, d:2 * d]
    mq = z[:, 2 * d:]

    xb_ref[hist:hist + tm, :] = xb
    cw = cw_ref[...]
    xc = xb_ref[hist - 3:hist - 3 + tm, :] * cw[0:1]
    xc = xc + xb_ref[hist - 2:hist - 2 + tm, :] * cw[1:2]
    xc = xc + xb_ref[hist - 1:hist - 1 + tm, :] * cw[2:3]
    xc = xc + xb * cw[3:4]
    xc = xc + cb_ref[...]
    tail = xb_ref[tm + hist - 3:tm + hist, :]
    conv_ref[...] = tail
    xb_ref[hist - 3:hist, :] = tail

    sp = _softplus(-lam_ref[...])
    _rg_gates(xc, wa_ref, ba_ref[...], wx_ref, bx_ref[...], sp, a_ref, u_ref)

    row = lax.broadcasted_iota(jnp.int32, (SUBLANES, d), 0)

    def group(g, h):
        r0 = pl.multiple_of(g * SUBLANES, SUBLANES)
        a = a_ref[pl.ds(r0, SUBLANES), :]
        b = u_ref[pl.ds(r0, SUBLANES), :]
        for s in (1, 2, 4):
            keep = row >= s
            b = jnp.where(keep, a * pltpu.roll(b, s, axis=0) + b, b)
            a = jnp.where(keep, a * pltpu.roll(a, s, axis=0), a)
        hs = a * h + b
        u_ref[pl.ds(r0, SUBLANES), :] = hs
        return hs[SUBLANES - 1:SUBLANES, :]

    h_end = lax.fori_loop(0, tm // SUBLANES, group, h_ref[...])
    h_ref[...] = h_end
    hlast_ref[...] = h_end

    yrg_ref[...] = (u_ref[...] * _gelu_tanh(gate)).astype(BF16)
    ymem_ref[...] = _shared_mem_attend(mq, mk_ref[...], mv_ref[...]).astype(BF16)


def _rg_prompt(x, g, w_in, cw, cb, wa, ba, wx, bx, lam, mk, mv, *, tm):
    s, d = x.shape
    rg_in = w_in.shape[1]
    row = lambda i: (i, 0)
    return pl.pallas_call(
        functools.partial(_rg_prompt_kernel, tm=tm),
        grid=(s // tm,),
        in_specs=[pl.BlockSpec((tm, d), row),
                  _const_spec((1, d)), _const_spec((d, rg_in)),
                  _const_spec((CONV_W, d)), _const_spec((1, d)),
                  _const_spec(wa.shape), _const_spec((1, d)),
                  _const_spec(wx.shape), _const_spec((1, d)),
                  _const_spec((1, d)),
                  _const_spec(mk.shape), _const_spec(mv.shape)],
        out_specs=[pl.BlockSpec((tm, d), row),
                   pl.BlockSpec((tm, MEM_W), row),
                   pl.BlockSpec((1, d), lambda i: (0, 0)),
                   pl.BlockSpec((CONV_W - 1, d), lambda i: (0, 0))],
        out_shape=[jax.ShapeDtypeStruct((s, d), BF16),
                   jax.ShapeDtypeStruct((s, MEM_W), BF16),
                   jax.ShapeDtypeStruct((1, d), F32),
                   jax.ShapeDtypeStruct((CONV_W - 1, d), F32)],
        scratch_shapes=[pltpu.VMEM((tm + SUBLANES, d), F32),
                        pltpu.VMEM((tm, d), F32),
                        pltpu.VMEM((tm, d), F32),
                        pltpu.VMEM((1, d), F32)],
        compiler_params=_params(),
        name="rg_prompt",
    )(x, g, w_in, cw, cb, wa, ba, wx, bx, lam, mk, mv)


def _rg_sample_kernel(x_ref, g_ref, win_ref, cw_ref, cb_ref, wa_ref, ba_ref, wx_ref, bx_ref,
                      lam_ref, prev_ref, h0_ref,
                      yrg_ref, mq_ref, hnew_ref, conv_ref, a_ref, u_ref):
    d = a_ref.shape[1]
    z = _bdot(_rms(x_ref[...], g_ref[...]), win_ref[...])
    gate = z[:, :d]
    xb = z[:, d:2 * d]
    mq_ref[...] = z[:, 2 * d:]

    cw = cw_ref[...]
    xc = prev_ref[0] * cw[0:1]
    xc = xc + prev_ref[1] * cw[1:2]
    xc = xc + prev_ref[2] * cw[2:3]
    xc = xc + xb * cw[3:4]
    xc = xc + cb_ref[...]
    conv_ref[0] = prev_ref[1]
    conv_ref[1] = prev_ref[2]
    conv_ref[2] = xb

    sp = _softplus(-lam_ref[...])
    _rg_gates(xc, wa_ref, ba_ref[...], wx_ref, bx_ref[...], sp, a_ref, u_ref)
    h = u_ref[...] + a_ref[...] * h0_ref[...]
    hnew_ref[...] = h
    yrg_ref[...] = (h * _gelu_tanh(gate)).astype(BF16)


def _rg_sample(x, g, w_in, cw, cb, wa, ba, wx, bx, lam, prev, h0):
    b, d = x.shape
    args = (x, g, w_in, cw, cb, wa, ba, wx, bx, lam, prev, h0)
    return pl.pallas_call(
        _rg_sample_kernel,
        grid=(1,),
        in_specs=[_const_spec(a.shape) for a in args],
        out_specs=[_const_spec((b, d)), _const_spec((b, MEM_W)), _const_spec((b, d)),
                   _const_spec((CONV_W - 1, b, d))],
        out_shape=[jax.ShapeDtypeStruct((b, d), BF16),
                   jax.ShapeDtypeStruct((b, MEM_W), F32),
                   jax.ShapeDtypeStruct((b, d), F32),
                   jax.ShapeDtypeStruct((CONV_W - 1, b, d), F32)],
        scratch_shapes=[pltpu.VMEM((b, d), F32), pltpu.VMEM((b, d), F32)],
        compiler_params=_params(),
        name="rg_sample",
    )(*args)


MEM_Q_ROWS = 2 * SUBLANES


def _mem_sample_kernel(q_ref, k_ref, v_ref, o_ref):
    g, rows, _ = k_ref.shape
    q = (q_ref[...] * MEM_SCALE).astype(BF16)
    s = lax.dot_general(q, k_ref[...].astype(BF16), BATCH_NT, preferred_element_type=F32)
    q_head = lax.broadcasted_iota(jnp.int32, (g, MEM_Q_ROWS, rows), 1)
    k_head = lax.broadcasted_iota(jnp.int32, (g, MEM_Q_ROWS, rows), 2) & (MEM_HEADS - 1)
    s = jnp.where(q_head == k_head, s, NEG_BIG)
    p = jnp.exp(s - jnp.max(s, axis=-1, keepdims=True))
    l = jnp.sum(p, axis=-1, keepdims=True)
    o_ref[...] = lax.dot_general(p.astype(BF16), v_ref[...].astype(BF16), BATCH_NN,
                                 preferred_element_type=F32) / l


def _mem_sample(mq, cache_k, cache_v, layer, *, group):
    depth, b, n_mem, heads, hd = cache_k.shape
    rows = n_mem * heads
    k = cache_k.reshape(depth * b, rows, hd)
    v = cache_v.reshape(depth * b, rows, hd)
    q = jnp.pad(mq.reshape(b, heads, hd), ((0, 0), (0, MEM_Q_ROWS - heads), (0, 0)))
    base = layer * (b // group)
    out = pl.pallas_call(
        _mem_sample_kernel,
        grid=(b // group,),
        in_specs=[pl.BlockSpec((group, MEM_Q_ROWS, hd), lambda i: (i, 0, 0)),
                  pl.BlockSpec((group, rows, hd), lambda i: (base + i, 0, 0)),
                  pl.BlockSpec((group, rows, hd), lambda i: (base + i, 0, 0))],
        out_specs=pl.BlockSpec((group, MEM_Q_ROWS, hd), lambda i: (i, 0, 0)),
        out_shape=jax.ShapeDtypeStruct((b, MEM_Q_ROWS, hd), F32),
        compiler_params=_params(),
        name="mem_sample",
    )(q, k, v)
    return out[:, :heads, :].reshape(b, heads * hd)


def _mla_pre_kernel(x_ref, g_ref, win_ref, qn_ref, kvn_ref, wqn_ref, wqp_ref, wqs_ref, wuk_ref,
                    inv_ref, sign_ref, *rest, tm, pos_base, pos_stride, prompt):
    if prompt:
        mk_ref, mv_ref, ckv_ref, kpe_ref, kshared_ref, q_ref, mem_ref, kheads_ref = rest
    else:
        ckv_ref, kpe_ref, kshared_ref, q_ref, mem_ref = rest
    o_kv = Q_LORA
    o_mq = Q_LORA + KV_LORA
    o_pe = o_mq + MEM_W
    o_ps = o_pe + ROPE_PAD

    z = _bdot(_rms(x_ref[...], g_ref[...]), win_ref[...])
    cq = _rms(z[:, :o_kv], qn_ref[...])
    ckv = _rms(z[:, o_kv:o_mq], kvn_ref[...])
    mq = z[:, o_mq:o_pe]

    t = pl.program_id(0) * tm + lax.broadcasted_iota(jnp.int32, (tm, ROPE_PAD), 0)
    pos = (pos_base + pos_stride * t).astype(F32)
    ang = pos * inv_ref[...]
    cos = jnp.cos(ang)
    sin = jnp.sin(ang) * sign_ref[...]

    kpe = z[:, o_pe:o_ps] * cos + z[:, o_ps:] * sin
    ckv_ref[...] = ckv
    kpe_ref[...] = kpe[:, :QK_ROPE]
    ckv_b = ckv.astype(BF16)
    kpe_b = kpe.astype(BF16)

    cqb = cq.astype(BF16)
    q_nope = jnp.dot(cqb, wqn_ref[...], preferred_element_type=F32)
    q_rope = jnp.dot(cqb, wqp_ref[...], preferred_element_type=F32)
    q_swap = jnp.dot(cqb, wqs_ref[...], preferred_element_type=F32)
    if prompt:
        kshared_ref[...] = ckv_b
        k_nope = jnp.dot(ckv_b, wuk_ref[...], preferred_element_type=F32)
    else:
        kshared_ref[...] = jnp.concatenate([ckv_b, kpe_b], axis=-1)
    for h in range(MLA_HEADS):
        ns = slice(h * QK_NOPE, (h + 1) * QK_NOPE)
        ps = slice(h * ROPE_PAD, (h + 1) * ROPE_PAD)
        q_pe = q_rope[:, ps] * cos + q_swap[:, ps] * sin
        if prompt:
            q_main = q_nope[:, ns]
            kheads_ref[h] = jnp.concatenate([k_nope[:, ns].astype(BF16), kpe_b], axis=-1)
        else:
            q_main = _bdot(q_nope[:, ns], wuk_ref[h])
        q_scale = MLA_SCALE * LOG2E if prompt else MLA_SCALE
        q_ref[h] = (jnp.concatenate([q_main, q_pe], axis=-1) * q_scale).astype(BF16)

    if prompt:
        mem_ref[...] = _shared_mem_attend(mq, mk_ref[...], mv_ref[...]).astype(mem_ref.dtype)
    else:
        mem_ref[...] = mq


def _mla_pre(x, g, w_in, qn, kvn, wqn, wqp, wqs, wuk, inv, sign, mem_kv, *, tm, pos_base, pos_stride):
    s, d = x.shape
    prompt = mem_kv is not None
    row = lambda i: (i, 0)
    heads_row = lambda i: (0, i, 0)
    consts = [g, w_in, qn, kvn, wqn, wqp, wqs, wuk, inv, sign] + (list(mem_kv) if prompt else [])
    qw = QK_NOPE + ROPE_PAD if prompt else KCAT
    kw = KV_LORA if prompt else KCAT
    out_specs = [pl.BlockSpec((tm, KV_LORA), row),
                 pl.BlockSpec((tm, QK_ROPE), row),
                 pl.BlockSpec((tm, kw), row),
                 pl.BlockSpec((MLA_HEADS, tm, qw), heads_row),
                 pl.BlockSpec((tm, MEM_W), row)]
    out_shape = [jax.ShapeDtypeStruct((s, KV_LORA), F32),
                 jax.ShapeDtypeStruct((s, QK_ROPE), F32),
                 jax.ShapeDtypeStruct((s, kw), BF16),
                 jax.ShapeDtypeStruct((MLA_HEADS, s, qw), BF16),
                 jax.ShapeDtypeStruct((s, MEM_W), BF16 if prompt else F32)]
    if prompt:
        out_specs.append(pl.BlockSpec((MLA_HEADS, tm, qw), heads_row))
        out_shape.append(jax.ShapeDtypeStruct((MLA_HEADS, s, qw), BF16))
    return pl.pallas_call(
        functools.partial(_mla_pre_kernel, tm=tm, pos_base=pos_base, pos_stride=pos_stride, prompt=prompt),
        grid=(s // tm,),
        in_specs=[pl.BlockSpec((tm, d), row)] + [_const_spec(a.shape) for a in consts],
        out_specs=out_specs,
        out_shape=out_shape,
        compiler_params=_params(),
        name="mla_pre",
    )(x, *consts)


def _mla_attn_kernel(pt_ref, q_ref, k_ref, v_ref, wuv_ref, qs_ref, knew_ref, ckv_hbm, kpe_hbm,
                     y_ref, os_ref,
                     m_ref, l_ref, acc_ref, s_ref,
                     ckv_buf, kpe_buf, sems, kc_ref, sc_ref, ms_ref, ls_ref, accs_ref, cnt_ref,
                     *, t, chunk_pages):
    h = pl.program_id(0)
    i = pl.program_id(1)
    n_seq = qs_ref.shape[0]
    total = pt_ref.shape[0] // chunk_pages
    n_c = total // n_seq
    chunk = chunk_pages * PAGE_SIZE

    def cache_start(g, par):
        first_page = (g % total) * chunk_pages
        for p in range(chunk_pages):
            page = pt_ref[first_page + p]
            pltpu.make_async_copy(ckv_hbm.at[page], ckv_buf.at[par, p], sems.at[0, par]).start()
            pltpu.make_async_copy(kpe_hbm.at[page], kpe_buf.at[par, p], sems.at[1, par]).start()

    def cache_wait(par):
        all_pages = pl.ds(0, chunk_pages)
        pltpu.make_async_copy(ckv_hbm.at[all_pages], ckv_buf.at[par], sems.at[0, par]).wait()
        pltpu.make_async_copy(kpe_hbm.at[all_pages], kpe_buf.at[par], sems.at[1, par]).wait()

    def cache_scores(g, par):
        q = qs_ref[(g % total) // n_c]
        q_pe = q[:, KV_LORA:KV_LORA + QK_ROPE]
        kc_ref[par] = ckv_buf[par].reshape(chunk, KV_LORA).astype(BF16)
        s = lax.dot_general(q[:, :KV_LORA], kc_ref[par], NT_DIMS, preferred_element_type=F32)
        sc_ref[par] = s + jnp.concatenate(
            [_bdot(q_pe, kpe_buf[par, p].astype(BF16)) for p in range(chunk_pages)], axis=-1)

    def cache_update(g, par):
        b = g // n_c
        first = (g % n_c) == 0
        q = qs_ref[b]
        s = sc_ref[par]
        m_prev = jnp.where(first, NEG_BIG, ms_ref[...])
        l_prev = jnp.where(first, 0.0, ls_ref[...])
        acc_prev = jnp.where(first, 0.0, accs_ref[...])
        m_new = jnp.maximum(m_prev, jnp.max(s, axis=-1, keepdims=True))
        alpha = jnp.exp(m_prev - m_new)
        p = jnp.exp(s - jnp.tile(m_new, (1, chunk // LANES)))
        p_sum = p[:, :LANES]
        for c in range(1, chunk // LANES):
            p_sum = p_sum + p[:, c * LANES:(c + 1) * LANES]
        l = alpha * l_prev + p_sum
        acc = jnp.tile(alpha, (1, KV_LORA // LANES)) * acc_prev + _bdot(p, kc_ref[par])
        ms_ref[...] = m_new
        ls_ref[...] = l
        accs_ref[...] = acc
        kn = knew_ref[pl.ds(b, 1), :]
        s_new = jnp.sum(q.astype(F32) * kn, axis=-1, keepdims=True)
        m_old = m_new[:, :1]
        m_all = jnp.maximum(m_old, s_new)
        scale_old = jnp.exp(m_old - m_all)
        p_new = jnp.exp(s_new - m_all)
        l_all = scale_old * jnp.sum(l, axis=-1, keepdims=True) + p_new
        os_ref[b] = (scale_old * acc + p_new.astype(BF16).astype(F32) * kn[:, :KV_LORA]) / l_all

    @pl.when((h == 0) & (i == 0))
    def _():
        cnt_ref[0] = 0
        cache_start(0, 0)
        cache_start(1, 1)
        cache_wait(0)
        cache_scores(0, 0)
        cache_start(2, 0)

    m_ref[...] = jnp.full_like(m_ref, NEG_BIG)
    l_ref[...] = jnp.zeros_like(l_ref)
    acc_ref[...] = jnp.zeros_like(acc_ref)

    def scores(k_start, slot):
        s_ref[slot] = lax.dot_general(q_ref[...], k_ref[pl.ds(k_start, t), :], NT_DIMS,
                                      preferred_element_type=F32)

    def update(k_start, slot, masked):
        s = s_ref[slot]
        if masked:
            q_pos = lax.broadcasted_iota(jnp.int32, (t, t), 0)
            k_pos = lax.broadcasted_iota(jnp.int32, (t, t), 1)
            s = jnp.where(k_pos <= q_pos, s, NEG_BIG)
        m_prev = m_ref[...]
        m_new = jnp.maximum(m_prev, jnp.max(s, axis=-1, keepdims=True))
        alpha = jnp.exp2(m_prev - m_new)
        p = jnp.exp2(s - jnp.tile(m_new, (1, t // LANES)))
        p_sum = p[:, :LANES]
        for c in range(1, t // LANES):
            p_sum = p_sum + p[:, c * LANES:(c + 1) * LANES]
        l_ref[...] = alpha * l_ref[...] + p_sum
        pv = _bdot(p, v_ref[pl.ds(k_start, t), :])
        acc_ref[...] = jnp.tile(alpha, (1, KV_LORA // LANES)) * acc_ref[...] + pv
        m_ref[...] = m_new

    scores(0, 0)
    g0 = cnt_ref[0]
    n_pairs = i // 2
    n_with = jnp.minimum(n_pairs, (total - g0) // 2)

    def pair_with_cache(jj, c):
        g = g0 + 2 * jj
        k0 = pl.multiple_of(2 * jj * t, t)
        cache_wait(1)
        scores(k0 + t, 1)
        update(k0, 0, False)
        cache_update(g, 0)
        cache_scores(g + 1, 1)
        cache_start(g + 3, 1)
        cache_wait(0)
        scores(k0 + 2 * t, 0)
        update(k0 + t, 1, False)
        cache_update(g + 1, 1)
        cache_scores(g + 2, 0)
        cache_start(g + 4, 0)
        return c

    def pair(jj, c):
        k0 = pl.multiple_of(2 * jj * t, t)
        scores(k0 + t, 1)
        update(k0, 0, False)
        scores(k0 + 2 * t, 0)
        update(k0 + t, 1, False)
        return c

    lax.fori_loop(0, n_with, pair_with_cache, 0)
    lax.fori_loop(n_with, n_pairs, pair, 0)
    g1 = g0 + 2 * n_with
    cnt_ref[0] = g1

    @pl.when((g0 < total) & (g1 == total))
    def _():
        cache_wait(1)
        cache_wait(0)

    k_diag = pl.multiple_of(i * t, t)

    @pl.when(i % 2 == 0)
    def _():
        update(k_diag, 0, True)

    @pl.when(i % 2 == 1)
    def _():
        scores(k_diag, 1)
        update(k_diag - t, 0, False)
        update(k_diag, 1, True)

    o = acc_ref[...] / jnp.sum(l_ref[...], axis=-1, keepdims=True)
    y_ref[...] = _bdot(o, wuv_ref[...]).astype(y_ref.dtype)


def _mla_attn(q_heads, k_heads, v, wuv, page_table, qs, knew, cache_ckv, kpe_pages, *, t, chunk_pages):
    heads, s, w = q_heads.shape
    n_seq, n_pages = page_table.shape
    n_tiles = s // t
    total = n_seq * (n_pages // chunk_pages)
    assert total % 2 == 0 and 2 * heads * sum(i // 2 for i in range(n_tiles)) >= total
    chunk = chunk_pages * PAGE_SIZE
    grid_spec = pltpu.PrefetchScalarGridSpec(
        num_scalar_prefetch=1,
        grid=(heads, n_tiles),
        in_specs=[pl.BlockSpec((None, t, w), lambda h, i, pt: (h, i, 0)),
                  pl.BlockSpec((None, s, w), lambda h, i, pt: (h, 0, 0)),
                  _const_spec(v.shape),
                  pl.BlockSpec((None, KV_LORA, V_HEAD), lambda h, i, pt: (h, 0, 0)),
                  _const_spec(qs.shape), _const_spec(knew.shape),
                  pl.BlockSpec(memory_space=pl.ANY), pl.BlockSpec(memory_space=pl.ANY)],
        out_specs=[pl.BlockSpec((t, V_HEAD), lambda h, i, pt: (i, h)),
                   pl.BlockSpec((n_seq, heads, KV_LORA), lambda h, i, pt: (0, 0, 0))],
        scratch_shapes=[pltpu.VMEM((t, LANES), F32), pltpu.VMEM((t, LANES), F32),
                        pltpu.VMEM((t, KV_LORA), F32), pltpu.VMEM((2, t, t), F32),
                        pltpu.VMEM((2, chunk_pages, PAGE_SIZE, KV_LORA), F32),
                        pltpu.VMEM((2, chunk_pages, QK_ROPE, PAGE_SIZE), F32),
                        pltpu.SemaphoreType.DMA((2, 2)),
                        pltpu.VMEM((2, chunk, KV_LORA), BF16), pltpu.VMEM((2, heads, chunk), F32),
                        pltpu.VMEM((heads, LANES), F32), pltpu.VMEM((heads, LANES), F32),
                        pltpu.VMEM((heads, KV_LORA), F32),
                        pltpu.SMEM((1,), jnp.int32)],
    )
    return pl.pallas_call(
        functools.partial(_mla_attn_kernel, t=t, chunk_pages=chunk_pages),
        grid_spec=grid_spec,
        out_shape=[jax.ShapeDtypeStruct((s, heads * V_HEAD), BF16),
                   jax.ShapeDtypeStruct((n_seq, heads, KV_LORA), F32)],
        compiler_params=pltpu.CompilerParams(dimension_semantics=("arbitrary", "arbitrary"),
                                             vmem_limit_bytes=VMEM_LIMIT_ATTN),
        name="mla_attn",
    )(page_table.reshape(-1), q_heads, k_heads, v, wuv, qs, knew, cache_ckv, kpe_pages)


def _uv_kernel(o_ref, wuv_ref, y_ref):
    for h in range(MLA_HEADS):
        y_ref[:, h * V_HEAD:(h + 1) * V_HEAD] = _bdot(o_ref[h], wuv_ref[h]).astype(y_ref.dtype)


def _uv_project(o_hm, wuv):
    heads, b, _ = o_hm.shape
    return pl.pallas_call(
        _uv_kernel,
        grid=(1,),
        in_specs=[_const_spec(o_hm.shape), _const_spec(wuv.shape)],
        out_specs=_const_spec((b, heads * V_HEAD)),
        out_shape=jax.ShapeDtypeStruct((b, heads * V_HEAD), BF16),
        compiler_params=_params(),
        name="uv_project",
    )(o_hm, wuv)


def _out_ffn_kernel(x_ref, ya_ref, ym_ref, wo_ref, gpost_ref, gpre_ref, wgu_ref, wd_ref, gfpost_ref, o_ref):
    da = ya_ref.shape[1]
    dff = wd_ref.shape[0]
    mix = _bdot(ya_ref[...], wo_ref[:da, :]) + _bdot(ym_ref[...], wo_ref[da:, :])
    x1 = x_ref[...] + _rms(mix, gpost_ref[...])
    hb = _rms(x1, gpre_ref[...]).astype(BF16)
    ffn = None
    for c0 in range(0, dff, FFN_CHUNK):
        c1 = min(c0 + FFN_CHUNK, dff)
        gate = jnp.dot(hb, wgu_ref[:, c0:c1], preferred_element_type=F32)
        up = jnp.dot(hb, wgu_ref[:, dff + c0:dff + c1], preferred_element_type=F32)
        part = _bdot((gate * jax.nn.sigmoid(gate)) * up, wd_ref[c0:c1, :])
        ffn = part if ffn is None else ffn + part
    o_ref[...] = x1 + _rms(ffn, gfpost_ref[...])


def _layer_spec(stacked, layer):
    tail = stacked.shape[1:]
    return pl.BlockSpec((None,) + tail, lambda *_: (layer,) + (0,) * len(tail), pipeline_mode=pl.Buffered(1))


def _out_ffn(x, ya, ym, wo, gpost, gpre, wgu, wd, gfpost, layer, *, tm):
    s, d = x.shape
    row = lambda i: (i, 0)
    consts = (wo, gpost, gpre, wgu, wd, gfpost)
    const_specs = [_const_spec(wo.shape), _const_spec(gpost.shape), _const_spec(gpre.shape),
                   _layer_spec(wgu, layer), _layer_spec(wd, layer), _const_spec(gfpost.shape)]
    return pl.pallas_call(
        _out_ffn_kernel,
        grid=(s // tm,),
        in_specs=[pl.BlockSpec((tm, d), row),
                  pl.BlockSpec((tm, ya.shape[1]), row),
                  pl.BlockSpec((tm, ym.shape[1]), row)] + const_specs,
        out_specs=pl.BlockSpec((tm, d), row),
        out_shape=jax.ShapeDtypeStruct((s, d), F32),
        compiler_params=_params(),
        name="out_ffn",
    )(x, ya, ym, *consts)


def _pad_cols(w, width):
    return jnp.pad(w, ((0, 0), (0, width - w.shape[1])))


def _swap_halves(w):
    half = w.shape[-1] // 2
    return jnp.concatenate([w[..., half:], w[..., :half]], axis=-1)


def _mla_weights(w_in, w_q_up, w_uk, w_uv):
    o2, o3 = Q_LORA + KV_LORA, Q_LORA + KV_LORA + QK_ROPE
    w_pe = w_in[:, o2:o3]
    w_in_ext = jnp.concatenate(
        [w_in[:, :o2], w_in[:, o3:], _pad_cols(w_pe, ROPE_PAD), _pad_cols(_swap_halves(w_pe), ROPE_PAD)],
        axis=1).astype(BF16)
    wq_nope = w_q_up[:, :, :QK_NOPE].reshape(Q_LORA, MLA_HEADS * QK_NOPE).astype(BF16)
    wq_pe = w_q_up[:, :, QK_NOPE:]
    pad = ((0, 0), (0, 0), (0, ROPE_PAD - QK_ROPE))
    wq_rope = jnp.pad(wq_pe, pad).reshape(Q_LORA, MLA_HEADS * ROPE_PAD).astype(BF16)
    wq_swap = jnp.pad(_swap_halves(wq_pe), pad).reshape(Q_LORA, MLA_HEADS * ROPE_PAD).astype(BF16)
    wuk_t = jnp.transpose(w_uk, (1, 2, 0)).astype(BF16)
    wuk_flat = w_uk.reshape(KV_LORA, MLA_HEADS * QK_NOPE).astype(BF16)
    wuv_t = jnp.transpose(w_uv, (1, 0, 2)).astype(BF16)
    return w_in_ext, wq_nope, wq_rope, wq_swap, wuk_t, wuk_flat, wuv_t


def kernel(x_prompt, x_sample, mem_prompt, state_rg_h, state_rg_conv, cache_ckv, cache_kpe, cache_mem_k, cache_mem_v, page_table, norm_mix_pre, norm_mix_post, norm_ffn_pre, norm_ffn_post, norm_mem, w_mem_kv, w_ffn_gate_up, w_ffn_down, rg_w_in, rg_conv_w, rg_conv_b, rg_gate_a_w, rg_gate_a_b, rg_gate_x_w, rg_gate_x_b, rg_lambda, rg_w_out, mla_w_in, mla_q_norm, mla_kv_norm, mla_w_q_up, mla_w_uk, mla_w_uv, mla_w_out):
    bp, sp, d = x_prompt.shape
    bs, ss, _ = x_sample.shape
    depth = norm_mix_pre.shape[0]
    n_mem = mem_prompt.shape[1]
    assert bp == 1 and ss == 1 and depth == 2
    past = page_table.shape[1] * PAGE_SIZE
    tm = FFN_ROW_TILE
    tm_mix = MIXER_ROW_TILE

    row = lambda a, l: a[l].reshape(1, -1)
    xp = x_prompt.reshape(sp, d)
    xs = x_sample.reshape(bs, d)

    mk_p, mv_p = _memkv(mem_prompt.reshape(n_mem, d), norm_mem.reshape(depth, 1, d), w_mem_kv.astype(BF16))

    half = jnp.arange(0, QK_ROPE, 2, dtype=F32) / QK_ROPE
    inv = ROPE_THETA ** (-half)
    zeros = jnp.zeros((ROPE_PAD - QK_ROPE,), F32)
    inv_pad = jnp.concatenate([inv, inv, zeros]).reshape(1, ROPE_PAD)
    sign_pad = jnp.concatenate([-jnp.ones_like(inv), jnp.ones_like(inv), zeros]).reshape(1, ROPE_PAD)

    wgu_all = w_ffn_gate_up.astype(BF16)
    wd_all = w_ffn_down.astype(BF16)

    def ffn_args(l):
        return (row(norm_mix_post, l), row(norm_ffn_pre, l), wgu_all, wd_all, row(norm_ffn_post, l), l)

    rg = (row(norm_mix_pre, 0), rg_w_in[0].astype(BF16), rg_conv_w[0], row(rg_conv_b, 0),
          rg_gate_a_w[0].astype(BF16), row(rg_gate_a_b, 0), rg_gate_x_w[0].astype(BF16), row(rg_gate_x_b, 0),
          row(rg_lambda, 0))
    wo0 = rg_w_out[0].astype(BF16)
    yrg_p, ymem_p, p_h, p_conv = _rg_prompt(xp, *rg, mk_p[0], mv_p[0], tm=tm_mix)
    xp = _out_ffn(xp, yrg_p, ymem_p, wo0, *ffn_args(0), tm=tm)

    prev_s = jnp.transpose(state_rg_conv[0], (1, 0, 2))
    yrg_s, mq_s, s_h, s_conv = _rg_sample(xs, *rg, prev_s, state_rg_h[0])
    ymem_s = _mem_sample(mq_s, cache_mem_k, cache_mem_v, 0, group=MEM_GROUP)
    xs = _out_ffn(xs, yrg_s, ymem_s, wo0, *ffn_args(0), tm=bs)

    w_in_ext, wq_nope, wq_rope, wq_swap, wuk_t, wuk_flat, wuv_t = _mla_weights(
        mla_w_in[0], mla_w_q_up[0], mla_w_uk[0], mla_w_uv[0])
    mla_head = (row(norm_mix_pre, 1), w_in_ext, row(mla_q_norm, 0), row(mla_kv_norm, 0), wq_nope, wq_rope, wq_swap)
    wo1 = mla_w_out[0].astype(BF16)

    s_ckv, s_kpe, kcat_s, qcat_s, mq_s = _mla_pre(xs, *mla_head, wuk_t, inv_pad, sign_pad, None,
                                                  tm=bs, pos_base=past, pos_stride=0)
    ymem_s = _mem_sample(mq_s, cache_mem_k, cache_mem_v, 1, group=MEM_GROUP)
    p_ckv, p_kpe, v_p, q_p, ymem_p, k_p = _mla_pre(xp, *mla_head, wuk_flat, inv_pad, sign_pad, (mk_p[1], mv_p[1]),
                                                   tm=tm_mix, pos_base=0, pos_stride=1)

    kpe_pages = jnp.swapaxes(cache_kpe[0], 1, 2)
    yatt_p, o_lat = _mla_attn(q_p, k_p, v_p, wuv_t, page_table, jnp.transpose(qcat_s, (1, 0, 2)),
                              kcat_s.astype(F32), cache_ckv[0], kpe_pages, t=ATTN_TILE,
                              chunk_pages=CACHE_CHUNK_PAGES)
    xp = _out_ffn(xp, yatt_p, ymem_p, wo1, *ffn_args(1), tm=tm)
    yatt_s = _uv_project(jnp.transpose(o_lat, (1, 0, 2)).astype(BF16), wuv_t)
    xs = _out_ffn(xs, yatt_s, ymem_s, wo1, *ffn_args(1), tm=bs)

    mem_shape = (depth, bp, n_mem, MEM_HEADS, MEM_HD)
    return (xp.reshape(bp, sp, d), xs.reshape(bs, ss, d),
            p_h.reshape(1, bp, d), p_conv.reshape(1, bp, CONV_W - 1, d),
            p_ckv.reshape(1, bp, sp, KV_LORA), p_kpe.reshape(1, bp, sp, QK_ROPE),
            mk_p.reshape(mem_shape), mv_p.reshape(mem_shape),
            s_h.reshape(1, bs, d), jnp.transpose(s_conv, (1, 0, 2)).reshape(1, bs, CONV_W - 1, d),
            s_ckv.reshape(1, bs, ss, KV_LORA), s_kpe.reshape(1, bs, ss, QK_ROPE))
```

```python
import functools

import jax
import jax.numpy as jnp
from jax import lax
from jax.experimental import pallas as pl
from jax.experimental.pallas import tpu as pltpu

F32 = jnp.float32
BF16 = jnp.bfloat16

EPS = 1e-6
RG_C = 8.0
RG_BLOCKS = 8
RG_BLOCK_W = 128
CONV_W = 4
MLA_HEADS = 8
Q_LORA = 512
KV_LORA = 256
QK_NOPE = 128
QK_ROPE = 64
V_HEAD = 128
ROPE_THETA = 10000.0
MLA_SCALE = (QK_NOPE + QK_ROPE) ** -0.5
MEM_HEADS = 4
MEM_HD = 128
MEM_W = MEM_HEADS * MEM_HD
MEM_SCALE = MEM_HD ** -0.5
PAGE_SIZE = 128

LANES = 128
SUBLANES = 8
ROPE_PAD = LANES
KCAT = KV_LORA + ROPE_PAD
VMEM_LIMIT = 56 * 1024 * 1024
VMEM_LIMIT_ATTN = 60 * 1024 * 1024
LOG2E = 1.4426950408889634
NEG_BIG = -1e30
FFN_CHUNK = 1024
FFN_ROW_TILE = 512
MIXER_ROW_TILE = 1024
ATTN_TILE = 1024
CACHE_CHUNK_PAGES = 32
MEM_GROUP = 8

NT_DIMS = (((1,), (1,)), ((), ()))
BATCH_NT = (((2,), (2,)), ((0,), (0,)))
BATCH_NN = (((2,), (1,)), ((0,), (0,)))


def _const_spec(shape):
    nd = len(shape)
    return pl.BlockSpec(shape, lambda *_: (0,) * nd, pipeline_mode=pl.Buffered(1))


def _params(n_axes=1):
    return pltpu.CompilerParams(dimension_semantics=("arbitrary",) * n_axes,
                                vmem_limit_bytes=VMEM_LIMIT)


def _rms(x, g):
    return x * lax.rsqrt(jnp.mean(x * x, axis=-1, keepdims=True) + EPS) * g


def _bdot(a, b):
    return jnp.dot(a.astype(BF16), b, preferred_element_type=F32)


def _gelu_tanh(x):
    return 0.5 * x * (1.0 + jnp.tanh(0.7978845608028654 * (x + 0.044715 * (x * x * x))))


def _softplus(x):
    return jnp.maximum(x, 0.0) + jnp.log1p(jnp.exp(-jnp.abs(x)))


def _shared_mem_attend(mq, mk, mv):
    outs = []
    for h in range(MEM_HEADS):
        hs = slice(h * MEM_HD, (h + 1) * MEM_HD)
        q = (mq[:, hs] * MEM_SCALE).astype(BF16)
        s = lax.dot_general(q, mk[:, hs].astype(BF16), NT_DIMS, preferred_element_type=F32)
        p = jnp.exp(s - jnp.max(s, axis=-1, keepdims=True))
        l = jnp.sum(p, axis=-1, keepdims=True)
        outs.append(_bdot(p, mv[:, hs].astype(BF16)) / l)
    return jnp.concatenate(outs, axis=-1)


def _memkv_kernel(mem_ref, g_ref, w_ref, mk_ref, mv_ref):
    kv = _bdot(_rms(mem_ref[...], g_ref[0]), w_ref[0])
    mk_ref[0] = kv[:, :MEM_W]
    mv_ref[0] = kv[:, MEM_W:]


def _memkv(mem, g, w):
    depth = w.shape[0]
    n_mem, d = mem.shape
    out = jax.ShapeDtypeStruct((depth, n_mem, MEM_W), F32)
    return pl.pallas_call(
        _memkv_kernel,
        grid=(depth,),
        in_specs=[_const_spec((n_mem, d)),
                  pl.BlockSpec((1, 1, d), lambda l: (l, 0, 0)),
                  pl.BlockSpec((1, d, 2 * MEM_W), lambda l: (l, 0, 0))],
        out_specs=[pl.BlockSpec((1, n_mem, MEM_W), lambda l: (l, 0, 0))] * 2,
        out_shape=[out, out],
        compiler_params=_params(),
        name="memkv",
    )(mem, g, w)


def _rg_gates(xc, wa_ref, ba, wx_ref, bx, sp, a_out, u_out):
    xcb = xc.astype(BF16)
    for n in range(RG_BLOCKS):
        bs = slice(n * RG_BLOCK_W, (n + 1) * RG_BLOCK_W)
        r = jax.nn.sigmoid(jnp.dot(xcb[:, bs], wa_ref[n], preferred_element_type=F32) + ba[:, bs])
        i = jax.nn.sigmoid(jnp.dot(xcb[:, bs], wx_ref[n], preferred_element_type=F32) + bx[:, bs])
        log_a = -RG_C * r * sp[:, bs]
        a = jnp.exp(log_a)
        a_out[:, bs] = a
        u_out[:, bs] = jnp.sqrt(-jnp.tanh(log_a) * (1.0 + a * a)) * (i * xc[:, bs])


def _rg_prompt_kernel(x_ref, g_ref, win_ref, cw_ref, cb_ref, wa_ref, ba_ref, wx_ref, bx_ref,
                      lam_ref, mk_ref, mv_ref,
                      yrg_ref, ymem_ref, hlast_ref, conv_ref,
                      xb_ref, a_ref, u_ref, h_ref, *, tm):
    d = a_ref.shape[1]
    hist = SUBLANES

    @pl.when(pl.program_id(0) == 0)
    def _():
        xb_ref[0:hist, :] = jnp.zeros((hist, d), F32)
        h_ref[...] = jnp.zeros_like(h_ref)

    z = _bdot(_rms(x_ref[...], g_ref[...]), win_ref[...])
    gate = z[:, :d]
    xb = z[:, d:2 * d]
    mq = z[:, 2 * d:]

    xb_ref[hist:hist + tm, :] = xb
    cw = cw_ref[...]
    xc = xb_ref[hist - 3:hist - 3 + tm, :] * cw[0:1]
    xc = xc + xb_ref[hist - 2:hist - 2 + tm, :] * cw[1:2]
    xc = xc + xb_ref[hist - 1:hist - 1 + tm, :] * cw[2:3]
    xc = xc + xb * cw[3:4]
    xc = xc + cb_ref[...]
    tail = xb_ref[tm + hist - 3:tm + hist, :]
    conv_ref[...] = tail
    xb_ref[hist - 3:hist, :] = tail

    sp = _softplus(-lam_ref[...])
    _rg_gates(xc, wa_ref, ba_ref[...], wx_ref, bx_ref[...], sp, a_ref, u_ref)

    row = lax.broadcasted_iota(jnp.int32, (SUBLANES, d), 0)

    def group(g, h):
        r0 = pl.multiple_of(g * SUBLANES, SUBLANES)
        a = a_ref[pl.ds(r0, SUBLANES), :]
        b = u_ref[pl.ds(r0, SUBLANES), :]
        for s in (1, 2, 4):
            keep = row >= s
            b = jnp.where(keep, a * pltpu.roll(b, s, axis=0) + b, b)
            a = jnp.where(keep, a * pltpu.roll(a, s, axis=0), a)
        hs = a * h + b
        u_ref[pl.ds(r0, SUBLANES), :] = hs
        return hs[SUBLANES - 1:SUBLANES, :]

    h_end = lax.fori_loop(0, tm // SUBLANES, group, h_ref[...])
    h_ref[...] = h_end
    hlast_ref[...] = h_end

    yrg_ref[...] = (u_ref[...] * _gelu_tanh(gate)).astype(BF16)
    ymem_ref[...] = _shared_mem_attend(mq, mk_ref[...], mv_ref[...]).astype(BF16)


def _rg_prompt(x, g, w_in, cw, cb, wa, ba, wx, bx, lam, mk, mv, *, tm):
    s, d = x.shape
    rg_in = w_in.shape[1]
    row = lambda i: (i, 0)
    return pl.pallas_call(
        functools.partial(_rg_prompt_kernel, tm=tm),
        grid=(s // tm,),
        in_specs=[pl.BlockSpec((tm, d), row),
                  _const_spec((1, d)), _const_spec((d, rg_in)),
                  _const_spec((CONV_W, d)), _const_spec((1, d)),
                  _const_spec(wa.shape), _const_spec((1, d)),
                  _const_spec(wx.shape), _const_spec((1, d)),
                  _const_spec((1, d)),
                  _const_spec(mk.shape), _const_spec(mv.shape)],
        out_specs=[pl.BlockSpec((tm, d), row),
                   pl.BlockSpec((tm, MEM_W), row),
                   pl.BlockSpec((1, d), lambda i: (0, 0)),
                   pl.BlockSpec((CONV_W - 1, d), lambda i: (0, 0))],
        out_shape=[jax.ShapeDtypeStruct((s, d), BF16),
                   jax.ShapeDtypeStruct((s, MEM_W), BF16),
                   jax.ShapeDtypeStruct((1, d), F32),
                   jax.ShapeDtypeStruct((CONV_W - 1, d), F32)],
        scratch_shapes=[pltpu.VMEM((tm + SUBLANES, d), F32),
                        pltpu.VMEM((tm, d), F32),
                        pltpu.VMEM((tm, d), F32),
                        pltpu.VMEM((1, d), F32)],
        compiler_params=_params(),
        name="rg_prompt",
    )(x, g, w_in, cw, cb, wa, ba, wx, bx, lam, mk, mv)


def _rg_sample_kernel(x_ref, g_ref, win_ref, cw_ref, cb_ref, wa_ref, ba_ref, wx_ref, bx_ref,
                      lam_ref, prev_ref, h0_ref,
                      yrg_ref, mq_ref, hnew_ref, conv_ref, a_ref, u_ref):
    d = a_ref.shape[1]
    z = _bdot(_rms(x_ref[...], g_ref[...]), win_ref[...])
    gate = z[:, :d]
    xb = z[:, d:2 * d]
    mq_ref[...] = z[:, 2 * d:]

    cw = cw_ref[...]
    xc = prev_ref[0] * cw[0:1]
    xc = xc + prev_ref[1] * cw[1:2]
    xc = xc + prev_ref[2] * cw[2:3]
    xc = xc + xb * cw[3:4]
    xc = xc + cb_ref[...]
    conv_ref[0] = prev_ref[1]
    conv_ref[1] = prev_ref[2]
    conv_ref[2] = xb

    sp = _softplus(-lam_ref[...])
    _rg_gates(xc, wa_ref, ba_ref[...], wx_ref, bx_ref[...], sp, a_ref, u_ref)
    h = u_ref[...] + a_ref[...] * h0_ref[...]
    hnew_ref[...] = h
    yrg_ref[...] = (h * _gelu_tanh(gate)).astype(BF16)


def _rg_sample(x, g, w_in, cw, cb, wa, ba, wx, bx, lam, prev, h0):
    b, d = x.shape
    args = (x, g, w_in, cw, cb, wa, ba, wx, bx, lam, prev, h0)
    return pl.pallas_call(
        _rg_sample_kernel,
        grid=(1,),
        in_specs=[_const_spec(a.shape) for a in args],
        out_specs=[_const_spec((b, d)), _const_spec((b, MEM_W)), _const_spec((b, d)),
                   _const_spec((CONV_W - 1, b, d))],
        out_shape=[jax.ShapeDtypeStruct((b, d), BF16),
                   jax.ShapeDtypeStruct((b, MEM_W), F32),
                   jax.ShapeDtypeStruct((b, d), F32),
                   jax.ShapeDtypeStruct((CONV_W - 1, b, d), F32)],
        scratch_shapes=[pltpu.VMEM((b, d), F32), pltpu.VMEM((b, d), F32)],
        compiler_params=_params(),
        name="rg_sample",
    )(*args)


MEM_Q_ROWS = 2 * SUBLANES


def _mem_sample_kernel(q_ref, k_ref, v_ref, o_ref):
    g, rows, _ = k_ref.shape
    q = (q_ref[...] * MEM_SCALE).astype(BF16)
    s = lax.dot_general(q, k_ref[...].astype(BF16), BATCH_NT, preferred_element_type=F32)
    q_head = lax.broadcasted_iota(jnp.int32, (g, MEM_Q_ROWS, rows), 1)
    k_head = lax.broadcasted_iota(jnp.int32, (g, MEM_Q_ROWS, rows), 2) & (MEM_HEADS - 1)
    s = jnp.where(q_head == k_head, s, NEG_BIG)
    p = jnp.exp(s - jnp.max(s, axis=-1, keepdims=True))
    l = jnp.sum(p, axis=-1, keepdims=True)
    o_ref[...] = lax.dot_general(p.astype(BF16), v_ref[...].astype(BF16), BATCH_NN,
                                 preferred_element_type=F32) / l


def _mem_sample(mq, cache_k, cache_v, layer, *, group):
    depth, b, n_mem, heads, hd = cache_k.shape
    rows = n_mem * heads
    k = cache_k.reshape(depth * b, rows, hd)
    v = cache_v.reshape(depth * b, rows, hd)
    q = jnp.pad(mq.reshape(b, heads, hd), ((0, 0), (0, MEM_Q_ROWS - heads), (0, 0)))
    base = layer * (b // group)
    out = pl.pallas_call(
        _mem_sample_kernel,
        grid=(b // group,),
        in_specs=[pl.BlockSpec((group, MEM_Q_ROWS, hd), lambda i: (i, 0, 0)),
                  pl.BlockSpec((group, rows, hd), lambda i: (base + i, 0, 0)),
                  pl.BlockSpec((group, rows, hd), lambda i: (base + i, 0, 0))],
        out_specs=pl.BlockSpec((group, MEM_Q_ROWS, hd), lambda i: (i, 0, 0)),
        out_shape=jax.ShapeDtypeStruct((b, MEM_Q_ROWS, hd), F32),
        compiler_params=_params(),
        name="mem_sample",
    )(q, k, v)
    return out[:, :heads, :].reshape(b, heads * hd)


def _mla_pre_kernel(x_ref, g_ref, win_ref, qn_ref, kvn_ref, wqn_ref, wqp_ref, wqs_ref, wuk_ref,
                    inv_ref, sign_ref, *rest, tm, pos_base, pos_stride, prompt):
    if prompt:
        mk_ref, mv_ref, ckv_ref, kpe_ref, kshared_ref, q_ref, mem_ref, kheads_ref = rest
    else:
        ckv_ref, kpe_ref, kshared_ref, q_ref, mem_ref = rest
    o_kv = Q_LORA
    o_mq = Q_LORA + KV_LORA
    o_pe = o_mq + MEM_W
    o_ps = o_pe + ROPE_PAD

    z = _bdot(_rms(x_ref[...], g_ref[...]), win_ref[...])
    cq = _rms(z[:, :o_kv], qn_ref[...])
    ckv = _rms(z[:, o_kv:o_mq], kvn_ref[...])
    mq = z[:, o_mq:o_pe]

    t = pl.program_id(0) * tm + lax.broadcasted_iota(jnp.int32, (tm, ROPE_PAD), 0)
    pos = (pos_base + pos_stride * t).astype(F32)
    ang = pos * inv_ref[...]
    cos = jnp.cos(ang)
    sin = jnp.sin(ang) * sign_ref[...]

    kpe = z[:, o_pe:o_ps] * cos + z[:, o_ps:] * sin
    ckv_ref[...] = ckv
    kpe_ref[...] = kpe[:, :QK_ROPE]
    ckv_b = ckv.astype(BF16)
    kpe_b = kpe.astype(BF16)

    cqb = cq.astype(BF16)
    q_nope = jnp.dot(cqb, wqn_ref[...], preferred_element_type=F32)
    q_rope = jnp.dot(cqb, wqp_ref[...], preferred_element_type=F32)
    q_swap = jnp.dot(cqb, wqs_ref[...], preferred_element_type=F32)
    if prompt:
        kshared_ref[...] = ckv_b
        k_nope = jnp.dot(ckv_b, wuk_ref[...], preferred_element_type=F32)
    else:
        kshared_ref[...] = jnp.concatenate([ckv_b, kpe_b], axis=-1)
    for h in range(MLA_HEADS):
        ns = slice(h * QK_NOPE, (h + 1) * QK_NOPE)
        ps = slice(h * ROPE_PAD, (h + 1) * ROPE_PAD)
        q_pe = q_rope[:, ps] * cos + q_swap[:, ps] * sin
        if prompt:
            q_main = q_nope[:, ns]
            kheads_ref[h] = jnp.concatenate([k_nope[:, ns].astype(BF16), kpe_b], axis=-1)
        else:
            q_main = _bdot(q_nope[:, ns], wuk_ref[h])
        q_scale = MLA_SCALE * LOG2E if prompt else MLA_SCALE
        q_ref[h] = (jnp.concatenate([q_main, q_pe], axis=-1) * q_scale).astype(BF16)

    if prompt:
        mem_ref[...] = _shared_mem_attend(mq, mk_ref[...], mv_ref[...]).astype(mem_ref.dtype)
    else:
        mem_ref[...] = mq


def _mla_pre(x, g, w_in, qn, kvn, wqn, wqp, wqs, wuk, inv, sign, mem_kv, *, tm, pos_base, pos_stride):
    s, d = x.shape
    prompt = mem_kv is not None
    row = lambda i: (i, 0)
    heads_row = lambda i: (0, i, 0)
    consts = [g, w_in, qn, kvn, wqn, wqp, wqs, wuk, inv, sign] + (list(mem_kv) if prompt else [])
    qw = QK_NOPE + ROPE_PAD if prompt else KCAT
    kw = KV_LORA if prompt else KCAT
    out_specs = [pl.BlockSpec((tm, KV_LORA), row),
                 pl.BlockSpec((tm, QK_ROPE), row),
                 pl.BlockSpec((tm, kw), row),
                 pl.BlockSpec((MLA_HEADS, tm, qw), heads_row),
                 pl.BlockSpec((tm, MEM_W), row)]
    out_shape = [jax.ShapeDtypeStruct((s, KV_LORA), F32),
                 jax.ShapeDtypeStruct((s, QK_ROPE), F32),
                 jax.ShapeDtypeStruct((s, kw), BF16),
                 jax.ShapeDtypeStruct((MLA_HEADS, s, qw), BF16),
                 jax.ShapeDtypeStruct((s, MEM_W), BF16 if prompt else F32)]
    if prompt:
        out_specs.append(pl.BlockSpec((MLA_HEADS, tm, qw), heads_row))
        out_shape.append(jax.ShapeDtypeStruct((MLA_HEADS, s, qw), BF16))
    return pl.pallas_call(
        functools.partial(_mla_pre_kernel, tm=tm, pos_base=pos_base, pos_stride=pos_stride, prompt=prompt),
        grid=(s // tm,),
        in_specs=[pl.BlockSpec((tm, d), row)] + [_const_spec(a.shape) for a in consts],
        out_specs=out_specs,
        out_shape=out_shape,
        compiler_params=_params(),
        name="mla_pre",
    )(x, *consts)


def _mla_attn_kernel(pt_ref, q_ref, k_ref, v_ref, wuv_ref, qs_ref, knew_ref, ckv_hbm, kpe_hbm,
                     y_ref, os_ref,
                     m_ref, l_ref, acc_ref, s_ref,
                     ckv_buf, kpe_buf, sems, kc_ref, sc_ref, ms_ref, ls_ref, accs_ref, cnt_ref,
                     *, t, chunk_pages):
    h = pl.program_id(0)
    i = pl.program_id(1)
    n_seq = qs_ref.shape[0]
    total = pt_ref.shape[0] // chunk_pages
    n_c = total // n_seq
    chunk = chunk_pages * PAGE_SIZE

    def cache_start(g, par):
        first_page = (g % total) * chunk_pages
        for p in range(chunk_pages):
            page = pt_ref[first_page + p]
            pltpu.make_async_copy(ckv_hbm.at[page], ckv_buf.at[par, p], sems.at[0, par]).start()
            pltpu.make_async_copy(kpe_hbm.at[page], kpe_buf.at[par, p], sems.at[1, par]).start()

    def cache_wait(par):
        all_pages = pl.ds(0, chunk_pages)
        pltpu.make_async_copy(ckv_hbm.at[all_pages], ckv_buf.at[par], sems.at[0, par]).wait()
        pltpu.make_async_copy(kpe_hbm.at[all_pages], kpe_buf.at[par], sems.at[1, par]).wait()

    def cache_scores(g, par):
        q = qs_ref[(g % total) // n_c]
        q_pe = q[:, KV_LORA:KV_LORA + QK_ROPE]
        kc_ref[par] = ckv_buf[par].reshape(chunk, KV_LORA).astype(BF16)
        s = lax.dot_general(q[:, :KV_LORA], kc_ref[par], NT_DIMS, preferred_element_type=F32)
        sc_ref[par] = s + jnp.concatenate(
            [_bdot(q_pe, kpe_buf[par, p].astype(BF16)) for p in range(chunk_pages)], axis=-1)

    def cache_update(g, par):
        b = g // n_c
        first = (g % n_c) == 0
        q = qs_ref[b]
        s = sc_ref[par]
        m_prev = jnp.where(first, NEG_BIG, ms_ref[...])
        l_prev = jnp.where(first, 0.0, ls_ref[...])
        acc_prev = jnp.where(first, 0.0, accs_ref[...])
        m_new = jnp.maximum(m_prev, jnp.max(s, axis=-1, keepdims=True))
        alpha = jnp.exp(m_prev - m_new)
        p = jnp.exp(s - jnp.tile(m_new, (1, chunk // LANES)))
        p_sum = p[:, :LANES]
        for c in range(1, chunk // LANES):
            p_sum = p_sum + p[:, c * LANES:(c + 1) * LANES]
        l = alpha * l_prev + p_sum
        acc = jnp.tile(alpha, (1, KV_LORA // LANES)) * acc_prev + _bdot(p, kc_ref[par])
        ms_ref[...] = m_new
        ls_ref[...] = l
        accs_ref[...] = acc
        kn = knew_ref[pl.ds(b, 1), :]
        s_new = jnp.sum(q.astype(F32) * kn, axis=-1, keepdims=True)
        m_old = m_new[:, :1]
        m_all = jnp.maximum(m_old, s_new)
        scale_old = jnp.exp(m_old - m_all)
        p_new = jnp.exp(s_new - m_all)
        l_all = scale_old * jnp.sum(l, axis=-1, keepdims=True) + p_new
        os_ref[b] = (scale_old * acc + p_new.astype(BF16).astype(F32) * kn[:, :KV_LORA]) / l_all

    @pl.when((h == 0) & (i == 0))
    def _():
        cnt_ref[0] = 0
        cache_start(0, 0)
        cache_start(1, 1)
        cache_wait(0)
        cache_scores(0, 0)
        cache_start(2, 0)

    m_ref[...] = jnp.full_like(m_ref, NEG_BIG)
    l_ref[...] = jnp.zeros_like(l_ref)
    acc_ref[...] = jnp.zeros_like(acc_ref)

    def scores(k_start, slot):
        s_ref[slot] = lax.dot_general(q_ref[...], k_ref[pl.ds(k_start, t), :], NT_DIMS,
                                      preferred_element_type=F32)

    def update(k_start, slot, masked, split=False):
        s = s_ref[slot]
        if masked:
            q_pos = lax.broadcasted_iota(jnp.int32, (t, t), 0)
            k_pos = lax.broadcasted_iota(jnp.int32, (t, t), 1)
            s = jnp.where(k_pos <= q_pos, s, NEG_BIG)
        m_prev = m_ref[...]
        m_new = jnp.maximum(m_prev, jnp.max(s, axis=-1, keepdims=True))
        alpha = jnp.exp2(m_prev - m_new)
        width = t // 2 if split else t
        p_sum = None
        pv = None
        for c0 in range(0, t, width):
            p = jnp.exp2(s[:, c0:c0 + width] - jnp.tile(m_new, (1, width // LANES)))
            for c in range(width // LANES):
                part = p[:, c * LANES:(c + 1) * LANES]
                p_sum = part if p_sum is None else p_sum + part
            part_pv = _bdot(p, v_ref[pl.ds(k_start + c0, width), :])
            pv = part_pv if pv is None else pv + part_pv
        l_ref[...] = alpha * l_ref[...] + p_sum
        acc_ref[...] = jnp.tile(alpha, (1, KV_LORA // LANES)) * acc_ref[...] + pv
        m_ref[...] = m_new

    scores(0, 0)
    g0 = cnt_ref[0]
    n_pairs = i // 2
    n_with = jnp.minimum(n_pairs, (total - g0) // 2)

    def pair_with_cache(jj, c):
        g = g0 + 2 * jj
        k0 = pl.multiple_of(2 * jj * t, t)
        cache_wait(1)
        scores(k0 + t, 1)
        update(k0, 0, False)
        cache_update(g, 0)
        cache_scores(g + 1, 1)
        cache_start(g + 3, 1)
        cache_wait(0)
        scores(k0 + 2 * t, 0)
        update(k0 + t, 1, False)
        cache_update(g + 1, 1)
        cache_scores(g + 2, 0)
        cache_start(g + 4, 0)
        return c

    def pair(jj, c):
        k0 = pl.multiple_of(2 * jj * t, t)
        scores(k0 + t, 1)
        update(k0, 0, False, split=True)
        scores(k0 + 2 * t, 0)
        update(k0 + t, 1, False, split=True)
        return c

    def two_pairs(jq, c):
        k0 = pl.multiple_of((2 * n_with + 4 * jq) * t, t)
        for k in range(4):
            scores(k0 + (k + 1) * t, (k + 1) % 2)
            update(k0 + k * t, k % 2, False, split=True)
        return c

    n_double = (n_pairs - n_with) // 2
    lax.fori_loop(0, n_with, pair_with_cache, 0)
    lax.fori_loop(0, n_double, two_pairs, 0)
    lax.fori_loop(n_with + 2 * n_double, n_pairs, pair, 0)
    g1 = g0 + 2 * n_with
    cnt_ref[0] = g1

    @pl.when((g0 < total) & (g1 == total))
    def _():
        cache_wait(1)
        cache_wait(0)

    k_diag = pl.multiple_of(i * t, t)

    @pl.when(i % 2 == 0)
    def _():
        update(k_diag, 0, True)

    @pl.when(i % 2 == 1)
    def _():
        scores(k_diag, 1)
        update(k_diag - t, 0, False)
        update(k_diag, 1, True)

    o = acc_ref[...] / jnp.sum(l_ref[...], axis=-1, keepdims=True)
    y_ref[...] = _bdot(o, wuv_ref[...]).astype(y_ref.dtype)


def _mla_attn(q_heads, k_heads, v, wuv, page_table, qs, knew, cache_ckv, kpe_pages, *, t, chunk_pages):
    heads, s, w = q_heads.shape
    n_seq, n_pages = page_table.shape
    n_tiles = s // t
    total = n_seq * (n_pages // chunk_pages)
    assert total % 2 == 0 and 2 * heads * sum(i // 2 for i in range(n_tiles)) >= total
    chunk = chunk_pages * PAGE_SIZE
    grid_spec = pltpu.PrefetchScalarGridSpec(
        num_scalar_prefetch=1,
        grid=(heads, n_tiles),
        in_specs=[pl.BlockSpec((None, t, w), lambda h, i, pt: (h, i, 0)),
                  pl.BlockSpec((None, s, w), lambda h, i, pt: (h, 0, 0)),
                  _const_spec(v.shape),
                  pl.BlockSpec((None, KV_LORA, V_HEAD), lambda h, i, pt: (h, 0, 0)),
                  _const_spec(qs.shape), _const_spec(knew.shape),
                  pl.BlockSpec(memory_space=pl.ANY), pl.BlockSpec(memory_space=pl.ANY)],
        out_specs=[pl.BlockSpec((t, V_HEAD), lambda h, i, pt: (i, h)),
                   pl.BlockSpec((n_seq, heads, KV_LORA), lambda h, i, pt: (0, 0, 0))],
        scratch_shapes=[pltpu.VMEM((t, LANES), F32), pltpu.VMEM((t, LANES), F32),
                        pltpu.VMEM((t, KV_LORA), F32), pltpu.VMEM((2, t, t), F32),
                        pltpu.VMEM((2, chunk_pages, PAGE_SIZE, KV_LORA), F32),
                        pltpu.VMEM((2, chunk_pages, QK_ROPE, PAGE_SIZE), F32),
                        pltpu.SemaphoreType.DMA((2, 2)),
                        pltpu.VMEM((2, chunk, KV_LORA), BF16), pltpu.VMEM((2, heads, chunk), F32),
                        pltpu.VMEM((heads, LANES), F32), pltpu.VMEM((heads, LANES), F32),
                        pltpu.VMEM((heads, KV_LORA), F32),
                        pltpu.SMEM((1,), jnp.int32)],
    )
    return pl.pallas_call(
        functools.partial(_mla_attn_kernel, t=t, chunk_pages=chunk_pages),
        grid_spec=grid_spec,
        out_shape=[jax.ShapeDtypeStruct((s, heads * V_HEAD), BF16),
                   jax.ShapeDtypeStruct((n_seq, heads, KV_LORA), F32)],
        compiler_params=pltpu.CompilerParams(dimension_semantics=("arbitrary", "arbitrary"),
                                             vmem_limit_bytes=VMEM_LIMIT_ATTN),
        name="mla_attn",
    )(page_table.reshape(-1), q_heads, k_heads, v, wuv, qs, knew, cache_ckv, kpe_pages)


def _uv_kernel(o_ref, wuv_ref, y_ref):
    for h in range(MLA_HEADS):
        y_ref[:, h * V_HEAD:(h + 1) * V_HEAD] = _bdot(o_ref[h], wuv_ref[h]).astype(y_ref.dtype)


def _uv_project(o_hm, wuv):
    heads, b, _ = o_hm.shape
    return pl.pallas_call(
        _uv_kernel,
        grid=(1,),
        in_specs=[_const_spec(o_hm.shape), _const_spec(wuv.shape)],
        out_specs=_const_spec((b, heads * V_HEAD)),
        out_shape=jax.ShapeDtypeStruct((b, heads * V_HEAD), BF16),
        compiler_params=_params(),
        name="uv_project",
    )(o_hm, wuv)


def _out_ffn_kernel(x_ref, ya_ref, ym_ref, wo_ref, gpost_ref, gpre_ref, wgu_ref, wd_ref, gfpost_ref, o_ref):
    da = ya_ref.shape[1]
    dff = wd_ref.shape[0]
    mix = _bdot(ya_ref[...], wo_ref[:da, :]) + _bdot(ym_ref[...], wo_ref[da:, :])
    x1 = x_ref[...] + _rms(mix, gpost_ref[...])
    hb = _rms(x1, gpre_ref[...]).astype(BF16)
    ffn = None
    for c0 in range(0, dff, FFN_CHUNK):
        c1 = min(c0 + FFN_CHUNK, dff)
        gate = jnp.dot(hb, wgu_ref[:, c0:c1], preferred_element_type=F32)
        up = jnp.dot(hb, wgu_ref[:, dff + c0:dff + c1], preferred_element_type=F32)
        part = _bdot((gate * jax.nn.sigmoid(gate)) * up, wd_ref[c0:c1, :])
        ffn = part if ffn is None else ffn + part
    o_ref[...] = x1 + _rms(ffn, gfpost_ref[...])


def _layer_spec(stacked, layer):
    tail = stacked.shape[1:]
    return pl.BlockSpec((None,) + tail, lambda *_: (layer,) + (0,) * len(tail), pipeline_mode=pl.Buffered(1))


def _out_ffn(x, ya, ym, wo, gpost, gpre, wgu, wd, gfpost, layer, *, tm):
    s, d = x.shape
    row = lambda i: (i, 0)
    consts = (wo, gpost, gpre, wgu, wd, gfpost)
    const_specs = [_const_spec(wo.shape), _const_spec(gpost.shape), _const_spec(gpre.shape),
                   _layer_spec(wgu, layer), _layer_spec(wd, layer), _const_spec(gfpost.shape)]
    return pl.pallas_call(
        _out_ffn_kernel,
        grid=(s // tm,),
        in_specs=[pl.BlockSpec((tm, d), row),
                  pl.BlockSpec((tm, ya.shape[1]), row),
                  pl.BlockSpec((tm, ym.shape[1]), row)] + const_specs,
        out_specs=pl.BlockSpec((tm, d), row),
        out_shape=jax.ShapeDtypeStruct((s, d), F32),
        compiler_params=_params(),
        name="out_ffn",
    )(x, ya, ym, *consts)


def _pad_cols(w, width):
    return jnp.pad(w, ((0, 0), (0, width - w.shape[1])))


def _swap_halves(w):
    half = w.shape[-1] // 2
    return jnp.concatenate([w[..., half:], w[..., :half]], axis=-1)


def _mla_weights(w_in, w_q_up, w_uk, w_uv):
    o2, o3 = Q_LORA + KV_LORA, Q_LORA + KV_LORA + QK_ROPE
    w_pe = w_in[:, o2:o3]
    w_in_ext = jnp.concatenate(
        [w_in[:, :o2], w_in[:, o3:], _pad_cols(w_pe, ROPE_PAD), _pad_cols(_swap_halves(w_pe), ROPE_PAD)],
        axis=1).astype(BF16)
    wq_nope = w_q_up[:, :, :QK_NOPE].reshape(Q_LORA, MLA_HEADS * QK_NOPE).astype(BF16)
    wq_pe = w_q_up[:, :, QK_NOPE:]
    pad = ((0, 0), (0, 0), (0, ROPE_PAD - QK_ROPE))
    wq_rope = jnp.pad(wq_pe, pad).reshape(Q_LORA, MLA_HEADS * ROPE_PAD).astype(BF16)
    wq_swap = jnp.pad(_swap_halves(wq_pe), pad).reshape(Q_LORA, MLA_HEADS * ROPE_PAD).astype(BF16)
    wuk_t = jnp.transpose(w_uk, (1, 2, 0)).astype(BF16)
    wuk_flat = w_uk.reshape(KV_LORA, MLA_HEADS * QK_NOPE).astype(BF16)
    wuv_t = jnp.transpose(w_uv, (1, 0, 2)).astype(BF16)
    return w_in_ext, wq_nope, wq_rope, wq_swap, wuk_t, wuk_flat, wuv_t


def kernel(x_prompt, x_sample, mem_prompt, state_rg_h, state_rg_conv, cache_ckv, cache_kpe, cache_mem_k, cache_mem_v, page_table, norm_mix_pre, norm_mix_post, norm_ffn_pre, norm_ffn_post, norm_mem, w_mem_kv, w_ffn_gate_up, w_ffn_down, rg_w_in, rg_conv_w, rg_conv_b, rg_gate_a_w, rg_gate_a_b, rg_gate_x_w, rg_gate_x_b, rg_lambda, rg_w_out, mla_w_in, mla_q_norm, mla_kv_norm, mla_w_q_up, mla_w_uk, mla_w_uv, mla_w_out):
    bp, sp, d = x_prompt.shape
    bs, ss, _ = x_sample.shape
    depth = norm_mix_pre.shape[0]
    n_mem = mem_prompt.shape[1]
    assert bp == 1 and ss == 1 and depth == 2
    past = page_table.shape[1] * PAGE_SIZE
    tm = FFN_ROW_TILE
    tm_mix = MIXER_ROW_TILE

    row = lambda a, l: a[l].reshape(1, -1)
    xp = x_prompt.reshape(sp, d)
    xs = x_sample.reshape(bs, d)

    mk_p, mv_p = _memkv(mem_prompt.reshape(n_mem, d), norm_mem.reshape(depth, 1, d), w_mem_kv.astype(BF16))

    half = jnp.arange(0, QK_ROPE, 2, dtype=F32) / QK_ROPE
    inv = ROPE_THETA ** (-half)
    zeros = jnp.zeros((ROPE_PAD - QK_ROPE,), F32)
    inv_pad = jnp.concatenate([inv, inv, zeros]).reshape(1, ROPE_PAD)
    sign_pad = jnp.concatenate([-jnp.ones_like(inv), jnp.ones_like(inv), zeros]).reshape(1, ROPE_PAD)

    wgu_all = w_ffn_gate_up.astype(BF16)
    wd_all = w_ffn_down.astype(BF16)

    def ffn_args(l):
        return (row(norm_mix_post, l), row(norm_ffn_pre, l), wgu_all, wd_all, row(norm_ffn_post, l), l)

    rg = (row(norm_mix_pre, 0), rg_w_in[0].astype(BF16), rg_conv_w[0], row(rg_conv_b, 0),
          rg_gate_a_w[0].astype(BF16), row(rg_gate_a_b, 0), rg_gate_x_w[0].astype(BF16), row(rg_gate_x_b, 0),
          row(rg_lambda, 0))
    wo0 = rg_w_out[0].astype(BF16)
    yrg_p, ymem_p, p_h, p_conv = _rg_prompt(xp, *rg, mk_p[0], mv_p[0], tm=tm_mix)
    xp = _out_ffn(xp, yrg_p, ymem_p, wo0, *ffn_args(0), tm=tm)

    prev_s = jnp.transpose(state_rg_conv[0], (1, 0, 2))
    yrg_s, mq_s, s_h, s_conv = _rg_sample(xs, *rg, prev_s, state_rg_h[0])
    ymem_s = _mem_sample(mq_s, cache_mem_k, cache_mem_v, 0, group=MEM_GROUP)
    xs = _out_ffn(xs, yrg_s, ymem_s, wo0, *ffn_args(0), tm=bs)

    w_in_ext, wq_nope, wq_rope, wq_swap, wuk_t, wuk_flat, wuv_t = _mla_weights(
        mla_w_in[0], mla_w_q_up[0], mla_w_uk[0], mla_w_uv[0])
    mla_head = (row(norm_mix_pre, 1), w_in_ext, row(mla_q_norm, 0), row(mla_kv_norm, 0), wq_nope, wq_rope, wq_swap)
    wo1 = mla_w_out[0].astype(BF16)

    s_ckv, s_kpe, kcat_s, qcat_s, mq_s = _mla_pre(xs, *mla_head, wuk_t, inv_pad, sign_pad, None,
                                                  tm=bs, pos_base=past, pos_stride=0)
    ymem_s = _mem_sample(mq_s, cache_mem_k, cache_mem_v, 1, group=MEM_GROUP)
    p_ckv, p_kpe, v_p, q_p, ymem_p, k_p = _mla_pre(xp, *mla_head, wuk_flat, inv_pad, sign_pad, (mk_p[1], mv_p[1]),
                                                   tm=tm_mix, pos_base=0, pos_stride=1)

    kpe_pages = jnp.swapaxes(cache_kpe[0], 1, 2)
    yatt_p, o_lat = _mla_attn(q_p, k_p, v_p, wuv_t, page_table, jnp.transpose(qcat_s, (1, 0, 2)),
                              kcat_s.astype(F32), cache_ckv[0], kpe_pages, t=ATTN_TILE,
                              chunk_pages=CACHE_CHUNK_PAGES)
    xp = _out_ffn(xp, yatt_p, ymem_p, wo1, *ffn_args(1), tm=tm)
    yatt_s = _uv_project(jnp.transpose(o_lat, (1, 0, 2)).astype(BF16), wuv_t)
    xs = _out_ffn(xs, yatt_s, ymem_s, wo1, *ffn_args(1), tm=bs)

    mem_shape = (depth, bp, n_mem, MEM_HEADS, MEM_HD)
    return (xp.reshape(bp, sp, d), xs.reshape(bs, ss, d),
            p_h.reshape(1, bp, d), p_conv.reshape(1, bp, CONV_W - 1, d),
            p_ckv.reshape(1, bp, sp, KV_LORA), p_kpe.reshape(1, bp, sp, QK_ROPE),
            mk_p.reshape(mem_shape), mv_p.reshape(mem_shape),
            s_h.reshape(1, bs, d), jnp.transpose(s_conv, (1, 0, 2)).reshape(1, bs, CONV_W - 1, d),
            s_ckv.reshape(1, bs, ss, KV_LORA), s_kpe.reshape(1, bs, ss, QK_ROPE))
```

```python
import functools

import jax
import jax.numpy as jnp
from jax import lax
from jax.experimental import pallas as pl
from jax.experimental.pallas import tpu as pltpu

F32 = jnp.float32
BF16 = jnp.bfloat16

EPS = 1e-6
RG_C = 8.0
RG_BLOCKS = 8
RG_BLOCK_W = 128
CONV_W = 4
MLA_HEADS = 8
Q_LORA = 512
KV_LORA = 256
QK_NOPE = 128
QK_ROPE = 64
V_HEAD = 128
ROPE_THETA = 10000.0
MLA_SCALE = (QK_NOPE + QK_ROPE) ** -0.5
MEM_HEADS = 4
MEM_HD = 128
MEM_W = MEM_HEADS * MEM_HD
MEM_SCALE = MEM_HD ** -0.5
PAGE_SIZE = 128

LANES = 128
SUBLANES = 8
ROPE_PAD = LANES
KCAT = KV_LORA + ROPE_PAD
VMEM_LIMIT = 56 * 1024 * 1024
VMEM_LIMIT_ATTN = 60 * 1024 * 1024
LOG2E = 1.4426950408889634
NEG_BIG = -1e30
FFN_CHUNK = 1024
FFN_ROW_TILE = 512
MIXER_ROW_TILE = 1024
ATTN_TILE = 1024
CACHE_CHUNK_PAGES = 32
MEM_GROUP = 8

NT_DIMS = (((1,), (1,)), ((), ()))
BATCH_NT = (((2,), (2,)), ((0,), (0,)))
BATCH_NN = (((2,), (1,)), ((0,), (0,)))


def _const_spec(shape):
    nd = len(shape)
    return pl.BlockSpec(shape, lambda *_: (0,) * nd, pipeline_mode=pl.Buffered(1))


def _params(n_axes=1):
    return pltpu.CompilerParams(dimension_semantics=("arbitrary",) * n_axes,
                                vmem_limit_bytes=VMEM_LIMIT)


def _rms(x, g):
    return x * lax.rsqrt(jnp.mean(x * x, axis=-1, keepdims=True) + EPS) * g


def _bdot(a, b):
    return jnp.dot(a.astype(BF16), b, preferred_element_type=F32)


def _gelu_tanh(x):
    return 0.5 * x * (1.0 + jnp.tanh(0.7978845608028654 * (x + 0.044715 * (x * x * x))))


def _softplus(x):
    return jnp.maximum(x, 0.0) + jnp.log1p(jnp.exp(-jnp.abs(x)))


def _shared_mem_attend(mq, mk, mv):
    outs = []
    for h in range(MEM_HEADS):
        hs = slice(h * MEM_HD, (h + 1) * MEM_HD)
        q = (mq[:, hs] * MEM_SCALE).astype(BF16)
        s = lax.dot_general(q, mk[:, hs].astype(BF16), NT_DIMS, preferred_element_type=F32)
        p = jnp.exp(s - jnp.max(s, axis=-1, keepdims=True))
        l = jnp.sum(p, axis=-1, keepdims=True)
        outs.append(_bdot(p, mv[:, hs].astype(BF16)) / l)
    return jnp.concatenate(outs, axis=-1)


def _memkv_kernel(mem_ref, g_ref, w_ref, mk_ref, mv_ref):
    kv = _bdot(_rms(mem_ref[...], g_ref[0]), w_ref[0])
    mk_ref[0] = kv[:, :MEM_W]
    mv_ref[0] = kv[:, MEM_W:]


def _memkv(mem, g, w):
    depth = w.shape[0]
    n_mem, d = mem.shape
    out = jax.ShapeDtypeStruct((depth, n_mem, MEM_W), F32)
    return pl.pallas_call(
        _memkv_kernel,
        grid=(depth,),
        in_specs=[_const_spec((n_mem, d)),
                  pl.BlockSpec((1, 1, d), lambda l: (l, 0, 0)),
                  pl.BlockSpec((1, d, 2 * MEM_W), lambda l: (l, 0, 0))],
        out_specs=[pl.BlockSpec((1, n_mem, MEM_W), lambda l: (l, 0, 0))] * 2,
        out_shape=[out, out],
        compiler_params=_params(),
        name="memkv",
    )(mem, g, w)


def _rg_gates(xc, wa_ref, ba, wx_ref, bx, sp, a_out, u_out):
    xcb = xc.astype(BF16)
    for n in range(RG_BLOCKS):
        bs = slice(n * RG_BLOCK_W, (n + 1) * RG_BLOCK_W)
        r = jax.nn.sigmoid(jnp.dot(xcb[:, bs], wa_ref[n], preferred_element_type=F32) + ba[:, bs])
        i = jax.nn.sigmoid(jnp.dot(xcb[:, bs], wx_ref[n], preferred_element_type=F32) + bx[:, bs])
        log_a = -RG_C * r * sp[:, bs]
        a = jnp.exp(log_a)
        a_out[:, bs] = a
        u_out[:, bs] = jnp.sqrt(-jnp.tanh(log_a) * (1.0 + a * a)) * (i * xc[:, bs])


def _rg_prompt_kernel(x_ref, g_ref, win_ref, cw_ref, cb_ref, wa_ref, ba_ref, wx_ref, bx_ref,
                      lam_ref, mk_ref, mv_ref,
                      yrg_ref, ymem_ref, hlast_ref, conv_ref,
                      xb_ref, a_ref, u_ref, h_ref, *, tm):
    d = a_ref.shape[1]
    hist = SUBLANES

    @pl.when(pl.program_id(0) == 0)
    def _():
        xb_ref[0:hist, :] = jnp.zeros((hist, d), F32)
        h_ref[...] = jnp.zeros_like(h_ref)

    z = _bdot(_rms(x_ref[...], g_ref[...]), win_ref[...])
    gate = z[:, :d]
    xb = z[:, d:2 * d]
    mq = z[:, 2 * d:]

    xb_ref[hist:hist + tm, :] = xb
    cw = cw_ref[...]
    xc = xb_ref[hist - 3:hist - 3 + tm, :] * cw[0:1]
    xc = xc + xb_ref[hist - 2:hist - 2 + tm, :] * cw[1:2]
    xc = xc + xb_ref[hist - 1:hist - 1 + tm, :] * cw[2:3]
    xc = xc + xb * cw[3:4]
    xc = xc + cb_ref[...]
    tail = xb_ref[tm + hist - 3:tm + hist, :]
    conv_ref[...] = tail
    xb_ref[hist - 3:hist, :] = tail

    sp = _softplus(-lam_ref[...])
    _rg_gates(xc, wa_ref, ba_ref[...], wx_ref, bx_ref[...], sp, a_ref, u_ref)

    row = lax.broadcasted_iota(jnp.int32, (SUBLANES, d), 0)

    def group(g, h):
        r0 = pl.multiple_of(g * SUBLANES, SUBLANES)
        a = a_ref[pl.ds(r0, SUBLANES), :]
        b = u_ref[pl.ds(r0, SUBLANES), :]
        for s in (1, 2, 4):
            keep = row >= s
            b = jnp.where(keep, a * pltpu.roll(b, s, axis=0) + b, b)
            a = jnp.where(keep, a * pltpu.roll(a, s, axis=0), a)
        hs = a * h + b
        u_ref[pl.ds(r0, SUBLANES), :] = hs
        return hs[SUBLANES - 1:SUBLANES, :]

    h_end = lax.fori_loop(0, tm // SUBLANES, group, h_ref[...])
    h_ref[...] = h_end
    hlast_ref[...] = h_end

    yrg_ref[...] = (u_ref[...] * _gelu_tanh(gate)).astype(BF16)
    ymem_ref[...] = _shared_mem_attend(mq, mk_ref[...], mv_ref[...]).astype(BF16)


def _rg_prompt(x, g, w_in, cw, cb, wa, ba, wx, bx, lam, mk, mv, *, tm):
    s, d = x.shape
    rg_in = w_in.shape[1]
    row = lambda i: (i, 0)
    return pl.pallas_call(
        functools.partial(_rg_prompt_kernel, tm=tm),
        grid=(s // tm,),
        in_specs=[pl.BlockSpec((tm, d), row),
                  _const_spec((1, d)), _const_spec((d, rg_in)),
                  _const_spec((CONV_W, d)), _const_spec((1, d)),
                  _const_spec(wa.shape), _const_spec((1, d)),
                  _const_spec(wx.shape), _const_spec((1, d)),
                  _const_spec((1, d)),
                  _const_spec(mk.shape), _const_spec(mv.shape)],
        out_specs=[pl.BlockSpec((tm, d), row),
                   pl.BlockSpec((tm, MEM_W), row),
                   pl.BlockSpec((1, d), lambda i: (0, 0)),
                   pl.BlockSpec((CONV_W - 1, d), lambda i: (0, 0))],
        out_shape=[jax.ShapeDtypeStruct((s, d), BF16),
                   jax.ShapeDtypeStruct((s, MEM_W), BF16),
                   jax.ShapeDtypeStruct((1, d), F32),
                   jax.ShapeDtypeStruct((CONV_W - 1, d), F32)],
        scratch_shapes=[pltpu.VMEM((tm + SUBLANES, d), F32),
                        pltpu.VMEM((tm, d), F32),
                        pltpu.VMEM((tm, d), F32),
                        pltpu.VMEM((1, d), F32)],
        compiler_params=_params(),
        name="rg_prompt",
    )(x, g, w_in, cw, cb, wa, ba, wx, bx, lam, mk, mv)


def _rg_sample_kernel(x_ref, g_ref, win_ref, cw_ref, cb_ref, wa_ref, ba_ref, wx_ref, bx_ref,
                      lam_ref, prev_ref, h0_ref,
                      yrg_ref, mq_ref, hnew_ref, conv_ref, a_ref, u_ref):
    d = a_ref.shape[1]
    z = _bdot(_rms(x_ref[...], g_ref[...]), win_ref[...])
    gate = z[:, :d]
    xb = z[:, d:2 * d]
    mq_ref[...] = z[:, 2 * d:]

    cw = cw_ref[...]
    xc = prev_ref[0] * cw[0:1]
    xc = xc + prev_ref[1] * cw[1:2]
    xc = xc + prev_ref[2] * cw[2:3]
    xc = xc + xb * cw[3:4]
    xc = xc + cb_ref[...]
    conv_ref[0] = prev_ref[1]
    conv_ref[1] = prev_ref[2]
    conv_ref[2] = xb

    sp = _softplus(-lam_ref[...])
    _rg_gates(xc, wa_ref, ba_ref[...], wx_ref, bx_ref[...], sp, a_ref, u_ref)
    h = u_ref[...] + a_ref[...] * h0_ref[...]
    hnew_ref[...] = h
    yrg_ref[...] = (h * _gelu_tanh(gate)).astype(BF16)


def _rg_sample(x, g, w_in, cw, cb, wa, ba, wx, bx, lam, prev, h0):
    b, d = x.shape
    args = (x, g, w_in, cw, cb, wa, ba, wx, bx, lam, prev, h0)
    return pl.pallas_call(
        _rg_sample_kernel,
        grid=(1,),
        in_specs=[_const_spec(a.shape) for a in args],
        out_specs=[_const_spec((b, d)), _const_spec((b, MEM_W)), _const_spec((b, d)),
                   _const_spec((CONV_W - 1, b, d))],
        out_shape=[jax.ShapeDtypeStruct((b, d), BF16),
                   jax.ShapeDtypeStruct((b, MEM_W), F32),
                   jax.ShapeDtypeStruct((b, d), F32),
                   jax.ShapeDtypeStruct((CONV_W - 1, b, d), F32)],
        scratch_shapes=[pltpu.VMEM((b, d), F32), pltpu.VMEM((b, d), F32)],
        compiler_params=_params(),
        name="rg_sample",
    )(*args)


MEM_Q_ROWS = 2 * SUBLANES


def _mem_sample_kernel(q_ref, k_ref, v_ref, o_ref):
    g, rows, _ = k_ref.shape
    q = (q_ref[...] * MEM_SCALE).astype(BF16)
    s = lax.dot_general(q, k_ref[...].astype(BF16), BATCH_NT, preferred_element_type=F32)
    q_head = lax.broadcasted_iota(jnp.int32, (g, MEM_Q_ROWS, rows), 1)
    k_head = lax.broadcasted_iota(jnp.int32, (g, MEM_Q_ROWS, rows), 2) & (MEM_HEADS - 1)
    s = jnp.where(q_head == k_head, s, NEG_BIG)
    p = jnp.exp(s - jnp.max(s, axis=-1, keepdims=True))
    l = jnp.sum(p, axis=-1, keepdims=True)
    o_ref[...] = lax.dot_general(p.astype(BF16), v_ref[...].astype(BF16), BATCH_NN,
                                 preferred_element_type=F32) / l


def _mem_sample(mq, cache_k, cache_v, layer, *, group):
    depth, b, n_mem, heads, hd = cache_k.shape
    rows = n_mem * heads
    k = cache_k.reshape(depth * b, rows, hd)
    v = cache_v.reshape(depth * b, rows, hd)
    q = jnp.pad(mq.reshape(b, heads, hd), ((0, 0), (0, MEM_Q_ROWS - heads), (0, 0)))
    base = layer * (b // group)
    out = pl.pallas_call(
        _mem_sample_kernel,
        grid=(b // group,),
        in_specs=[pl.BlockSpec((group, MEM_Q_ROWS, hd), lambda i: (i, 0, 0)),
                  pl.BlockSpec((group, rows, hd), lambda i: (base + i, 0, 0)),
                  pl.BlockSpec((group, rows, hd), lambda i: (base + i, 0, 0))],
        out_specs=pl.BlockSpec((group, MEM_Q_ROWS, hd), lambda i: (i, 0, 0)),
        out_shape=jax.ShapeDtypeStruct((b, MEM_Q_ROWS, hd), F32),
        compiler_params=_params(),
        name="mem_sample",
    )(q, k, v)
    return out[:, :heads, :].reshape(b, heads * hd)


def _mla_pre_kernel(x_ref, g_ref, win_ref, qn_ref, kvn_ref, wqn_ref, wqp_ref, wqs_ref, wuk_ref,
                    inv_ref, sign_ref, *rest, tm, pos_base, pos_stride, prompt):
    if prompt:
        mk_ref, mv_ref, ckv_ref, kpe_ref, kshared_ref, q_ref, mem_ref, kheads_ref = rest
    else:
        ckv_ref, kpe_ref, kshared_ref, q_ref, mem_ref = rest
    o_kv = Q_LORA
    o_mq = Q_LORA + KV_LORA
    o_pe = o_mq + MEM_W
    o_ps = o_pe + ROPE_PAD

    z = _bdot(_rms(x_ref[...], g_ref[...]), win_ref[...])
    cq = _rms(z[:, :o_kv], qn_ref[...])
    ckv = _rms(z[:, o_kv:o_mq], kvn_ref[...])
    mq = z[:, o_mq:o_pe]

    t = pl.program_id(0) * tm + lax.broadcasted_iota(jnp.int32, (tm, ROPE_PAD), 0)
    pos = (pos_base + pos_stride * t).astype(F32)
    ang = pos * inv_ref[...]
    cos = jnp.cos(ang)
    sin = jnp.sin(ang) * sign_ref[...]

    kpe = z[:, o_pe:o_ps] * cos + z[:, o_ps:] * sin
    ckv_ref[...] = ckv
    kpe_ref[...] = kpe[:, :QK_ROPE]
    ckv_b = ckv.astype(BF16)
    kpe_b = kpe.astype(BF16)

    cqb = cq.astype(BF16)
    q_nope = jnp.dot(cqb, wqn_ref[...], preferred_element_type=F32)
    q_rope = jnp.dot(cqb, wqp_ref[...], preferred_element_type=F32)
    q_swap = jnp.dot(cqb, wqs_ref[...], preferred_element_type=F32)
    if prompt:
        kshared_ref[...] = ckv_b
        k_nope = jnp.dot(ckv_b, wuk_ref[...], preferred_element_type=F32)
    else:
        kshared_ref[...] = jnp.concatenate([ckv_b, kpe_b], axis=-1)
    for h in range(MLA_HEADS):
        ns = slice(h * QK_NOPE, (h + 1) * QK_NOPE)
        ps = slice(h * ROPE_PAD, (h + 1) * ROPE_PAD)
        q_pe = q_rope[:, ps] * cos + q_swap[:, ps] * sin
        if prompt:
            q_main = q_nope[:, ns]
            kheads_ref[h] = jnp.concatenate([k_nope[:, ns].astype(BF16), kpe_b], axis=-1)
        else:
            q_main = _bdot(q_nope[:, ns], wuk_ref[h])
        q_scale = MLA_SCALE * LOG2E if prompt else MLA_SCALE
        q_ref[h] = (jnp.concatenate([q_main, q_pe], axis=-1) * q_scale).astype(BF16)

    if prompt:
        mem_ref[...] = _shared_mem_attend(mq, mk_ref[...], mv_ref[...]).astype(mem_ref.dtype)
    else:
        mem_ref[...] = mq


def _mla_pre(x, g, w_in, qn, kvn, wqn, wqp, wqs, wuk, inv, sign, mem_kv, *, tm, pos_base, pos_stride):
    s, d = x.shape
    prompt = mem_kv is not None
    row = lambda i: (i, 0)
    heads_row = lambda i: (0, i, 0)
    consts = [g, w_in, qn, kvn, wqn, wqp, wqs, wuk, inv, sign] + (list(mem_kv) if prompt else [])
    qw = QK_NOPE + ROPE_PAD if prompt else KCAT
    kw = KV_LORA if prompt else KCAT
    out_specs = [pl.BlockSpec((tm, KV_LORA), row),
                 pl.BlockSpec((tm, QK_ROPE), row),
                 pl.BlockSpec((tm, kw), row),
                 pl.BlockSpec((MLA_HEADS, tm, qw), heads_row),
                 pl.BlockSpec((tm, MEM_W), row)]
    out_shape = [jax.ShapeDtypeStruct((s, KV_LORA), F32),
                 jax.ShapeDtypeStruct((s, QK_ROPE), F32),
                 jax.ShapeDtypeStruct((s, kw), BF16),
                 jax.ShapeDtypeStruct((MLA_HEADS, s, qw), BF16),
                 jax.ShapeDtypeStruct((s, MEM_W), BF16 if prompt else F32)]
    if prompt:
        out_specs.append(pl.BlockSpec((MLA_HEADS, tm, qw), heads_row))
        out_shape.append(jax.ShapeDtypeStruct((MLA_HEADS, s, qw), BF16))
    return pl.pallas_call(
        functools.partial(_mla_pre_kernel, tm=tm, pos_base=pos_base, pos_stride=pos_stride, prompt=prompt),
        grid=(s // tm,),
        in_specs=[pl.BlockSpec((tm, d), row)] + [_const_spec(a.shape) for a in consts],
        out_specs=out_specs,
        out_shape=out_shape,
        compiler_params=_params(),
        name="mla_pre",
    )(x, *consts)


def _mla_attn_kernel(pt_ref, q_ref, k_ref, v_ref, wuv_ref, qs_ref, knew_ref, ckv_hbm, kpe_hbm,
                     y_ref, os_ref,
                     m_ref, l_ref, acc_ref, s_ref,
                     ckv_buf, kpe_buf, sems, kc_ref, sc_ref, ms_ref, ls_ref, accs_ref, cnt_ref,
                     *, t, chunk_pages):
    h = pl.program_id(0)
    i = pl.program_id(1)
    n_seq = qs_ref.shape[0]
    total = pt_ref.shape[0] // chunk_pages
    n_c = total // n_seq
    chunk = chunk_pages * PAGE_SIZE

    def cache_start(g, par):
        first_page = (g % total) * chunk_pages
        for p in range(chunk_pages):
            page = pt_ref[first_page + p]
            pltpu.make_async_copy(ckv_hbm.at[page], ckv_buf.at[par, p], sems.at[0, par]).start(priority=p % 2)
            pltpu.make_async_copy(kpe_hbm.at[page], kpe_buf.at[par, p], sems.at[1, par]).start(priority=p % 2)

    def cache_wait(par):
        all_pages = pl.ds(0, chunk_pages)
        pltpu.make_async_copy(ckv_hbm.at[all_pages], ckv_buf.at[par], sems.at[0, par]).wait()
        pltpu.make_async_copy(kpe_hbm.at[all_pages], kpe_buf.at[par], sems.at[1, par]).wait()

    def cache_scores(g, par):
        q = qs_ref[(g % total) // n_c]
        q_pe = q[:, KV_LORA:KV_LORA + QK_ROPE]
        kc_ref[par] = ckv_buf[par].reshape(chunk, KV_LORA).astype(BF16)
        s = lax.dot_general(q[:, :KV_LORA], kc_ref[par], NT_DIMS, preferred_element_type=F32)
        sc_ref[par] = s + jnp.concatenate(
            [_bdot(q_pe, kpe_buf[par, p].astype(BF16)) for p in range(chunk_pages)], axis=-1)

    def cache_update(g, par):
        b = g // n_c
        first = (g % n_c) == 0
        q = qs_ref[b]
        s = sc_ref[par]
        m_prev = jnp.where(first, NEG_BIG, ms_ref[...])
        l_prev = jnp.where(first, 0.0, ls_ref[...])
        acc_prev = jnp.where(first, 0.0, accs_ref[...])
        m_new = jnp.maximum(m_prev, jnp.max(s, axis=-1, keepdims=True))
        alpha = jnp.exp(m_prev - m_new)
        p = jnp.exp(s - jnp.tile(m_new, (1, chunk // LANES)))
        p_sum = p[:, :LANES]
        for c in range(1, chunk // LANES):
            p_sum = p_sum + p[:, c * LANES:(c + 1) * LANES]
        l = alpha * l_prev + p_sum
        acc = jnp.tile(alpha, (1, KV_LORA // LANES)) * acc_prev + _bdot(p, kc_ref[par])
        ms_ref[...] = m_new
        ls_ref[...] = l
        accs_ref[...] = acc
        kn = knew_ref[pl.ds(b, 1), :]
        s_new = jnp.sum(q.astype(F32) * kn, axis=-1, keepdims=True)
        m_old = m_new[:, :1]
        m_all = jnp.maximum(m_old, s_new)
        scale_old = jnp.exp(m_old - m_all)
        p_new = jnp.exp(s_new - m_all)
        l_all = scale_old * jnp.sum(l, axis=-1, keepdims=True) + p_new
        os_ref[b] = (scale_old * acc + p_new.astype(BF16).astype(F32) * kn[:, :KV_LORA]) / l_all

    @pl.when((h == 0) & (i == 0))
    def _():
        cnt_ref[0] = 0
        cache_start(0, 0)
        cache_start(1, 1)
        cache_wait(0)
        cache_scores(0, 0)
        cache_start(2, 0)

    m_ref[...] = jnp.full_like(m_ref, NEG_BIG)
    l_ref[...] = jnp.zeros_like(l_ref)
    acc_ref[...] = jnp.zeros_like(acc_ref)

    def scores(k_start, slot):
        s_ref[slot] = lax.dot_general(q_ref[...], k_ref[pl.ds(k_start, t), :], NT_DIMS,
                                      preferred_element_type=F32)

    def update(k_start, slot, masked, split=False):
        s = s_ref[slot]
        if masked:
            q_pos = lax.broadcasted_iota(jnp.int32, (t, t), 0)
            k_pos = lax.broadcasted_iota(jnp.int32, (t, t), 1)
            s = jnp.where(k_pos <= q_pos, s, NEG_BIG)
        m_prev = m_ref[...]
        m_new = jnp.maximum(m_prev, jnp.max(s, axis=-1, keepdims=True))
        alpha = jnp.exp2(m_prev - m_new)
        width = t // 2 if split else t
        p_sum = None
        pv = None
        for c0 in range(0, t, width):
            p = jnp.exp2(s[:, c0:c0 + width] - jnp.tile(m_new, (1, width // LANES)))
            for c in range(width // LANES):
                part = p[:, c * LANES:(c + 1) * LANES]
                p_sum = part if p_sum is None else p_sum + part
            part_pv = _bdot(p, v_ref[pl.ds(k_start + c0, width), :])
            pv = part_pv if pv is None else pv + part_pv
        l_ref[...] = alpha * l_ref[...] + p_sum
        acc_ref[...] = jnp.tile(alpha, (1, KV_LORA // LANES)) * acc_ref[...] + pv
        m_ref[...] = m_new

    scores(0, 0)
    g0 = cnt_ref[0]
    n_pairs = i // 2
    n_with = jnp.minimum(n_pairs, (total - g0) // 2)

    def pair_with_cache(jj, c):
        g = g0 + 2 * jj
        k0 = pl.multiple_of(2 * jj * t, t)
        cache_wait(1)
        scores(k0 + t, 1)
        update(k0, 0, False)
        cache_update(g, 0)
        cache_scores(g + 1, 1)
        cache_start(g + 3, 1)
        cache_wait(0)
        scores(k0 + 2 * t, 0)
        update(k0 + t, 1, False)
        cache_update(g + 1, 1)
        cache_scores(g + 2, 0)
        cache_start(g + 4, 0)
        return c

    def pair(jj, c):
        k0 = pl.multiple_of(2 * jj * t, t)
        scores(k0 + t, 1)
        update(k0, 0, False, split=True)
        scores(k0 + 2 * t, 0)
        update(k0 + t, 1, False, split=True)
        return c

    def two_pairs(jq, c):
        k0 = pl.multiple_of((2 * n_with + 4 * jq) * t, t)
        for k in range(4):
            scores(k0 + (k + 1) * t, (k + 1) % 2)
            update(k0 + k * t, k % 2, False, split=True)
        return c

    n_double = (n_pairs - n_with) // 2
    lax.fori_loop(0, n_with, pair_with_cache, 0)
    lax.fori_loop(0, n_double, two_pairs, 0)
    lax.fori_loop(n_with + 2 * n_double, n_pairs, pair, 0)
    g1 = g0 + 2 * n_with
    cnt_ref[0] = g1

    @pl.when((g0 < total) & (g1 == total))
    def _():
        cache_wait(1)
        cache_wait(0)

    k_diag = pl.multiple_of(i * t, t)

    @pl.when(i % 2 == 0)
    def _():
        update(k_diag, 0, True)

    @pl.when(i % 2 == 1)
    def _():
        scores(k_diag, 1)
        update(k_diag - t, 0, False)
        update(k_diag, 1, True)

    o = acc_ref[...] / jnp.sum(l_ref[...], axis=-1, keepdims=True)
    y_ref[...] = _bdot(o, wuv_ref[...]).astype(y_ref.dtype)


def _mla_attn(q_heads, k_heads, v, wuv, page_table, qs, knew, cache_ckv, kpe_pages, *, t, chunk_pages):
    heads, s, w = q_heads.shape
    n_seq, n_pages = page_table.shape
    n_tiles = s // t
    total = n_seq * (n_pages // chunk_pages)
    assert total % 2 == 0 and 2 * heads * sum(i // 2 for i in range(n_tiles)) >= total
    chunk = chunk_pages * PAGE_SIZE
    grid_spec = pltpu.PrefetchScalarGridSpec(
        num_scalar_prefetch=1,
        grid=(heads, n_tiles),
        in_specs=[pl.BlockSpec((None, t, w), lambda h, i, pt: (h, i, 0)),
                  pl.BlockSpec((None, s, w), lambda h, i, pt: (h, 0, 0)),
                  _const_spec(v.shape),
                  pl.BlockSpec((None, KV_LORA, V_HEAD), lambda h, i, pt: (h, 0, 0)),
                  _const_spec(qs.shape), _const_spec(knew.shape),
                  pl.BlockSpec(memory_space=pl.ANY), pl.BlockSpec(memory_space=pl.ANY)],
        out_specs=[pl.BlockSpec((t, V_HEAD), lambda h, i, pt: (i, h)),
                   pl.BlockSpec((n_seq, heads, KV_LORA), lambda h, i, pt: (0, 0, 0))],
        scratch_shapes=[pltpu.VMEM((t, LANES), F32), pltpu.VMEM((t, LANES), F32),
                        pltpu.VMEM((t, KV_LORA), F32), pltpu.VMEM((2, t, t), F32),
                        pltpu.VMEM((2, chunk_pages, PAGE_SIZE, KV_LORA), F32),
                        pltpu.VMEM((2, chunk_pages, QK_ROPE, PAGE_SIZE), F32),
                        pltpu.SemaphoreType.DMA((2, 2)),
                        pltpu.VMEM((2, chunk, KV_LORA), BF16), pltpu.VMEM((2, heads, chunk), F32),
                        pltpu.VMEM((heads, LANES), F32), pltpu.VMEM((heads, LANES), F32),
                        pltpu.VMEM((heads, KV_LORA), F32),
                        pltpu.SMEM((1,), jnp.int32)],
    )
    return pl.pallas_call(
        functools.partial(_mla_attn_kernel, t=t, chunk_pages=chunk_pages),
        grid_spec=grid_spec,
        out_shape=[jax.ShapeDtypeStruct((s, heads * V_HEAD), BF16),
                   jax.ShapeDtypeStruct((n_seq, heads, KV_LORA), F32)],
        compiler_params=pltpu.CompilerParams(dimension_semantics=("arbitrary", "arbitrary"),
                                             vmem_limit_bytes=VMEM_LIMIT_ATTN),
        name="mla_attn",
    )(page_table.reshape(-1), q_heads, k_heads, v, wuv, qs, knew, cache_ckv, kpe_pages)


def _uv_kernel(o_ref, wuv_ref, y_ref):
    for h in range(MLA_HEADS):
        y_ref[:, h * V_HEAD:(h + 1) * V_HEAD] = _bdot(o_ref[h], wuv_ref[h]).astype(y_ref.dtype)


def _uv_project(o_hm, wuv):
    heads, b, _ = o_hm.shape
    return pl.pallas_call(
        _uv_kernel,
        grid=(1,),
        in_specs=[_const_spec(o_hm.shape), _const_spec(wuv.shape)],
        out_specs=_const_spec((b, heads * V_HEAD)),
        out_shape=jax.ShapeDtypeStruct((b, heads * V_HEAD), BF16),
        compiler_params=_params(),
        name="uv_project",
    )(o_hm, wuv)


def _out_ffn_kernel(x_ref, ya_ref, ym_ref, wo_ref, gpost_ref, gpre_ref, wgu_ref, wd_ref, gfpost_ref, o_ref):
    da = ya_ref.shape[1]
    dff = wd_ref.shape[0]
    mix = _bdot(ya_ref[...], wo_ref[:da, :]) + _bdot(ym_ref[...], wo_ref[da:, :])
    x1 = x_ref[...] + _rms(mix, gpost_ref[...])
    hb = _rms(x1, gpre_ref[...]).astype(BF16)
    ffn = None
    for c0 in range(0, dff, FFN_CHUNK):
        c1 = min(c0 + FFN_CHUNK, dff)
        gate = jnp.dot(hb, wgu_ref[:, c0:c1], preferred_element_type=F32)
        up = jnp.dot(hb, wgu_ref[:, dff + c0:dff + c1], preferred_element_type=F32)
        part = _bdot((gate * jax.nn.sigmoid(gate)) * up, wd_ref[c0:c1, :])
        ffn = part if ffn is None else ffn + part
    o_ref[...] = x1 + _rms(ffn, gfpost_ref[...])


def _layer_spec(stacked, layer):
    tail = stacked.shape[1:]
    return pl.BlockSpec((None,) + tail, lambda *_: (layer,) + (0,) * len(tail), pipeline_mode=pl.Buffered(1))


def _out_ffn(x, ya, ym, wo, gpost, gpre, wgu, wd, gfpost, layer, *, tm):
    s, d = x.shape
    row = lambda i: (i, 0)
    consts = (wo, gpost, gpre, wgu, wd, gfpost)
    const_specs = [_const_spec(wo.shape), _const_spec(gpost.shape), _const_spec(gpre.shape),
                   _layer_spec(wgu, layer), _layer_spec(wd, layer), _const_spec(gfpost.shape)]
    return pl.pallas_call(
        _out_ffn_kernel,
        grid=(s // tm,),
        in_specs=[pl.BlockSpec((tm, d), row),
                  pl.BlockSpec((tm, ya.shape[1]), row),
                  pl.BlockSpec((tm, ym.shape[1]), row)] + const_specs,
        out_specs=pl.BlockSpec((tm, d), row),
        out_shape=jax.ShapeDtypeStruct((s, d), F32),
        compiler_params=_params(),
        name="out_ffn",
    )(x, ya, ym, *consts)


def _pad_cols(w, width):
    return jnp.pad(w, ((0, 0), (0, width - w.shape[1])))


def _swap_halves(w):
    half = w.shape[-1] // 2
    return jnp.concatenate([w[..., half:], w[..., :half]], axis=-1)


def _mla_weights(w_in, w_q_up, w_uk, w_uv):
    o2, o3 = Q_LORA + KV_LORA, Q_LORA + KV_LORA + QK_ROPE
    w_pe = w_in[:, o2:o3]
    w_in_ext = jnp.concatenate(
        [w_in[:, :o2], w_in[:, o3:], _pad_cols(w_pe, ROPE_PAD), _pad_cols(_swap_halves(w_pe), ROPE_PAD)],
        axis=1).astype(BF16)
    wq_nope = w_q_up[:, :, :QK_NOPE].reshape(Q_LORA, MLA_HEADS * QK_NOPE).astype(BF16)
    wq_pe = w_q_up[:, :, QK_NOPE:]
    pad = ((0, 0), (0, 0), (0, ROPE_PAD - QK_ROPE))
    wq_rope = jnp.pad(wq_pe, pad).reshape(Q_LORA, MLA_HEADS * ROPE_PAD).astype(BF16)
    wq_swap = jnp.pad(_swap_halves(wq_pe), pad).reshape(Q_LORA, MLA_HEADS * ROPE_PAD).astype(BF16)
    wuk_t = jnp.transpose(w_uk, (1, 2, 0)).astype(BF16)
    wuk_flat = w_uk.reshape(KV_LORA, MLA_HEADS * QK_NOPE).astype(BF16)
    wuv_t = jnp.transpose(w_uv, (1, 0, 2)).astype(BF16)
    return w_in_ext, wq_nope, wq_rope, wq_swap, wuk_t, wuk_flat, wuv_t


def kernel(x_prompt, x_sample, mem_prompt, state_rg_h, state_rg_conv, cache_ckv, cache_kpe, cache_mem_k, cache_mem_v, page_table, norm_mix_pre, norm_mix_post, norm_ffn_pre, norm_ffn_post, norm_mem, w_mem_kv, w_ffn_gate_up, w_ffn_down, rg_w_in, rg_conv_w, rg_conv_b, rg_gate_a_w, rg_gate_a_b, rg_gate_x_w, rg_gate_x_b, rg_lambda, rg_w_out, mla_w_in, mla_q_norm, mla_kv_norm, mla_w_q_up, mla_w_uk, mla_w_uv, mla_w_out):
    bp, sp, d = x_prompt.shape
    bs, ss, _ = x_sample.shape
    depth = norm_mix_pre.shape[0]
    n_mem = mem_prompt.shape[1]
    assert bp == 1 and ss == 1 and depth == 2
    past = page_table.shape[1] * PAGE_SIZE
    tm = FFN_ROW_TILE
    tm_mix = MIXER_ROW_TILE

    row = lambda a, l: a[l].reshape(1, -1)
    xp = x_prompt.reshape(sp, d)
    xs = x_sample.reshape(bs, d)

    mk_p, mv_p = _memkv(mem_prompt.reshape(n_mem, d), norm_mem.reshape(depth, 1, d), w_mem_kv.astype(BF16))

    half = jnp.arange(0, QK_ROPE, 2, dtype=F32) / QK_ROPE
    inv = ROPE_THETA ** (-half)
    zeros = jnp.zeros((ROPE_PAD - QK_ROPE,), F32)
    inv_pad = jnp.concatenate([inv, inv, zeros]).reshape(1, ROPE_PAD)
    sign_pad = jnp.concatenate([-jnp.ones_like(inv), jnp.ones_like(inv), zeros]).reshape(1, ROPE_PAD)

    wgu_all = w_ffn_gate_up.astype(BF16)
    wd_all = w_ffn_down.astype(BF16)

    def ffn_args(l):
        return (row(norm_mix_post, l), row(norm_ffn_pre, l), wgu_all, wd_all, row(norm_ffn_post, l), l)

    rg = (row(norm_mix_pre, 0), rg_w_in[0].astype(BF16), rg_conv_w[0], row(rg_conv_b, 0),
          rg_gate_a_w[0].astype(BF16), row(rg_gate_a_b, 0), rg_gate_x_w[0].astype(BF16), row(rg_gate_x_b, 0),
          row(rg_lambda, 0))
    wo0 = rg_w_out[0].astype(BF16)
    yrg_p, ymem_p, p_h, p_conv = _rg_prompt(xp, *rg, mk_p[0], mv_p[0], tm=tm_mix)
    xp = _out_ffn(xp, yrg_p, ymem_p, wo0, *ffn_args(0), tm=tm)

    prev_s = jnp.transpose(state_rg_conv[0], (1, 0, 2))
    yrg_s, mq_s, s_h, s_conv = _rg_sample(xs, *rg, prev_s, state_rg_h[0])
    ymem_s = _mem_sample(mq_s, cache_mem_k, cache_mem_v, 0, group=MEM_GROUP)
    xs = _out_ffn(xs, yrg_s, ymem_s, wo0, *ffn_args(0), tm=bs)

    w_in_ext, wq_nope, wq_rope, wq_swap, wuk_t, wuk_flat, wuv_t = _mla_weights(
        mla_w_in[0], mla_w_q_up[0], mla_w_uk[0], mla_w_uv[0])
    mla_head = (row(norm_mix_pre, 1), w_in_ext, row(mla_q_norm, 0), row(mla_kv_norm, 0), wq_nope, wq_rope, wq_swap)
    wo1 = mla_w_out[0].astype(BF16)

    s_ckv, s_kpe, kcat_s, qcat_s, mq_s = _mla_pre(xs, *mla_head, wuk_t, inv_pad, sign_pad, None,
                                                  tm=bs, pos_base=past, pos_stride=0)
    ymem_s = _mem_sample(mq_s, cache_mem_k, cache_mem_v, 1, group=MEM_GROUP)
    p_ckv, p_kpe, v_p, q_p, ymem_p, k_p = _mla_pre(xp, *mla_head, wuk_flat, inv_pad, sign_pad, (mk_p[1], mv_p[1]),
                                                   tm=tm_mix, pos_base=0, pos_stride=1)

    kpe_pages = jnp.swapaxes(cache_kpe[0], 1, 2)
    yatt_p, o_lat = _mla_attn(q_p, k_p, v_p, wuv_t, page_table, jnp.transpose(qcat_s, (1, 0, 2)),
                              kcat_s.astype(F32), cache_ckv[0], kpe_pages, t=ATTN_TILE,
                              chunk_pages=CACHE_CHUNK_PAGES)
    xp = _out_ffn(xp, yatt_p, ymem_p, wo1, *ffn_args(1), tm=tm)
    yatt_s = _uv_project(jnp.transpose(o_lat, (1, 0, 2)).astype(BF16), wuv_t)
    xs = _out_ffn(xs, yatt_s, ymem_s, wo1, *ffn_args(1), tm=bs)

    mem_shape = (depth, bp, n_mem, MEM_HEADS, MEM_HD)
    return (xp.reshape(bp, sp, d), xs.reshape(bs, ss, d),
            p_h.reshape(1, bp, d), p_conv.reshape(1, bp, CONV_W - 1, d),
            p_ckv.reshape(1, bp, sp, KV_LORA), p_kpe.reshape(1, bp, sp, QK_ROPE),
            mk_p.reshape(mem_shape), mv_p.reshape(mem_shape),
            s_h.reshape(1, bs, d), jnp.transpose(s_conv, (1, 0, 2)).reshape(1, bs, CONV_W - 1, d),
            s_ckv.reshape(1, bs, ss, KV_LORA), s_kpe.reshape(1, bs, ss, QK_ROPE))
```
